```python
import jax, jax.numpy as jnp
from jax import lax
import numpy as np

D_MODEL = 1024
BATCH = 1
SEQ = 16384
DEPTH = 2

N_MIXERS = 2
EXPAND = 2
D_INNER = EXPAND * D_MODEL
RMS_EPS = 1e-6

GLA_HEADS = 4
GLA_DK = D_MODEL // 2
GLA_DV = D_INNER
GLA_HEAD_K = GLA_DK // GLA_HEADS
GLA_HEAD_V = GLA_DV // GLA_HEADS
GLA_GATE_RANK = 16
GLA_GATE_NORMALIZER = 16.0
GLA_CHUNK = 64
GLA_PROJ = 2 * GLA_DK + 2 * GLA_DV + GLA_GATE_RANK

SSD_HEAD_DIM = 64
SSD_HEADS = D_INNER // SSD_HEAD_DIM
SSD_GROUPS = 8
SSD_HEADS_PER_GROUP = SSD_HEADS // SSD_GROUPS
SSD_STATE = 128
SSD_CONV = 4
SSD_CHUNK = 64
SSD_CONV_DIM = D_INNER + 2 * SSD_GROUPS * SSD_STATE
SSD_PROJ = D_INNER + SSD_CONV_DIM + SSD_HEADS

N_GLA = (DEPTH + 1) // 2
N_SSD = DEPTH // 2

kernel_name = "hybrid_gla_mamba2_interleaved"


def rmsnorm(x, w):
    xf = x.astype(jnp.float32)
    return xf * lax.rsqrt(jnp.mean(xf * xf, axis=-1, keepdims=True) + RMS_EPS) * w.astype(jnp.float32)


def gla_mixer(h, w_in, w_gate_up, b_gate_up, w_head_norm, w_out):
    bsz, L, _ = h.shape
    nc = L // GLA_CHUNK
    proj = h @ w_in
    q, k, v, g, gk_low = jnp.split(
        proj, [GLA_DK, 2 * GLA_DK, 2 * GLA_DK + GLA_DV, 2 * GLA_DK + 2 * GLA_DV], axis=-1)
    log_a = jax.nn.log_sigmoid((gk_low @ w_gate_up + b_gate_up).astype(jnp.float32)) / GLA_GATE_NORMALIZER

    def to_chunks(t, d):
        return t.astype(jnp.float32).reshape(bsz, nc, GLA_CHUNK, GLA_HEADS, d).transpose(1, 0, 3, 2, 4)

    qc = to_chunks(q, GLA_HEAD_K) * (GLA_HEAD_K ** -0.5)
    kc = to_chunks(k, GLA_HEAD_K)
    vc = to_chunks(v, GLA_HEAD_V)
    ac = to_chunks(log_a, GLA_HEAD_K)
    causal = jnp.tril(jnp.ones((GLA_CHUNK, GLA_CHUNK), dtype=bool))[None, None, :, :, None]

    def step(S, inp):
        qi, ki, vi, ai = inp
        b = jnp.cumsum(ai, axis=-2)
        o_inter = jnp.einsum('bhcd,bhdv->bhcv', qi * jnp.exp(b), S)
        rel = jnp.where(causal, b[:, :, :, None, :] - b[:, :, None, :, :], -jnp.inf)
        attn = jnp.sum(qi[:, :, :, None, :] * ki[:, :, None, :, :] * jnp.exp(rel), axis=-1)
        o = o_inter + jnp.einsum('bhij,bhjv->bhiv', attn, vi)
        b_last = b[:, :, -1:, :]
        S = jnp.exp(b_last[:, :, 0, :, None]) * S + jnp.einsum(
            'bhjd,bhjv->bhdv', ki * jnp.exp(b_last - b), vi)
        return S, o

    S0 = jnp.zeros((bsz, GLA_HEADS, GLA_HEAD_K, GLA_HEAD_V), jnp.float32)
    _, o = lax.scan(step, S0, (qc, kc, vc, ac))
    o = o.transpose(1, 0, 3, 2, 4).reshape(bsz, L, GLA_HEADS, GLA_HEAD_V)
    o = rmsnorm(o, w_head_norm).reshape(bsz, L, GLA_DV) * jax.nn.silu(g.astype(jnp.float32))
    return o.astype(h.dtype) @ w_out


def ssd_mixer(h, w_in, conv_w, conv_b, dt_bias, a_log, d_skip, w_gate_norm, w_out):
    bsz, L, _ = h.shape
    nc = L // SSD_CHUNK
    G, HG, P, N, C = SSD_GROUPS, SSD_HEADS_PER_GROUP, SSD_HEAD_DIM, SSD_STATE, SSD_CHUNK
    proj = h @ w_in
    z, xbc, dt = jnp.split(proj, [D_INNER, D_INNER + SSD_CONV_DIM], axis=-1)
    xbc = lax.conv_general_dilated(
        xbc, conv_w[:, None, :].astype(xbc.dtype), window_strides=(1,), padding=[(SSD_CONV - 1, 0)],
        dimension_numbers=('NWC', 'WIO', 'NWC'), feature_group_count=SSD_CONV_DIM)
    xbc = jax.nn.silu(xbc + conv_b)
    xs, Bm, Cm = jnp.split(xbc, [D_INNER, D_INNER + G * N], axis=-1)
    xs = xs.astype(jnp.float32).reshape(bsz, nc, C, G, HG, P)
    Bm = Bm.astype(jnp.float32).reshape(bsz, nc, C, G, N)
    Cm = Cm.astype(jnp.float32).reshape(bsz, nc, C, G, N)
    dt = jax.nn.softplus(dt.astype(jnp.float32) + dt_bias.astype(jnp.float32)).reshape(bsz, nc, C, G, HG)
    A = -jnp.exp(a_log.astype(jnp.float32)).reshape(G, HG)
    a_cum = jnp.cumsum(jnp.moveaxis(dt * A, 2, -1), axis=-1)
    xdt = xs * dt[..., None]

    causal = jnp.tril(jnp.ones((C, C), dtype=bool))
    Lmat = jnp.exp(jnp.where(causal, a_cum[..., :, None] - a_cum[..., None, :], -jnp.inf))
    cb = jnp.einsum('bzlgn,bzsgn->bzgls', Cm, Bm)
    y_diag = jnp.einsum('bzghls,bzsghp->bzlghp', cb[:, :, :, None] * Lmat, xdt)

    decay_to_end = jnp.moveaxis(jnp.exp(a_cum[..., -1:] - a_cum), -1, 2)
    states = jnp.einsum('bzsgn,bzsghp->bzghpn', Bm, xdt * decay_to_end[..., None])
    chunk_decay = jnp.exp(a_cum[..., -1])

    def step(hs, inp):
        st, dec = inp
        return dec[..., None, None] * hs + st, hs

    h0 = jnp.zeros((bsz, G, HG, P, N), jnp.float32)
    _, h_in = lax.scan(step, h0, (jnp.moveaxis(states, 1, 0), jnp.moveaxis(chunk_decay, 1, 0)))
    h_in = jnp.moveaxis(h_in, 0, 1)
    decay_from_start = jnp.moveaxis(jnp.exp(a_cum), -1, 2)
    y_off = jnp.einsum('bzlgn,bzghpn->bzlghp', Cm, h_in) * decay_from_start[..., None]

    y = y_diag + y_off + xs * d_skip.astype(jnp.float32).reshape(G, HG)[..., None]
    y = y.reshape(bsz, L, D_INNER) * jax.nn.silu(z.astype(jnp.float32))
    y = rmsnorm(y, w_gate_norm)
    return y.astype(h.dtype) @ w_out


def setup_inputs(seed: int = 0) -> dict:
    key = jax.random.key(seed)
    ks = jax.random.split(key, 17)
    f32 = jnp.float32

    def normal(k, shape, scale):
        return jax.random.normal(k, shape, f32) * scale

    x = jax.random.normal(ks[0], (BATCH, SEQ, D_MODEL), f32)
    norm_w = 1.0 + normal(ks[1], (DEPTH, D_MODEL), 0.02)
    gla_in_proj = normal(ks[2], (N_GLA, D_MODEL, GLA_PROJ), D_MODEL ** -0.5)
    gla_gate_up = normal(ks[3], (N_GLA, GLA_GATE_RANK, GLA_DK), GLA_GATE_RANK ** -0.5)
    gla_gate_bias = normal(ks[4], (N_GLA, GLA_DK), 0.1)
    gla_head_norm = 1.0 + normal(ks[5], (N_GLA, GLA_HEAD_V), 0.02)
    gla_out_proj = normal(ks[6], (N_GLA, GLA_DV, D_MODEL), GLA_DV ** -0.5)
    ssd_in_proj = normal(ks[7], (N_SSD, D_MODEL, SSD_PROJ), D_MODEL ** -0.5)
    ssd_conv_w = normal(ks[8], (N_SSD, SSD_CONV, SSD_CONV_DIM), SSD_CONV ** -0.5)
    ssd_conv_b = normal(ks[9], (N_SSD, SSD_CONV_DIM), 0.02)
    dt0 = jnp.exp(jax.random.uniform(ks[10], (N_SSD, SSD_HEADS), f32, np.log(1e-3), np.log(1e-1)))
    ssd_dt_bias = dt0 + jnp.log(-jnp.expm1(-dt0))
    ssd_a_log = jnp.log(jax.random.uniform(ks[11], (N_SSD, SSD_HEADS), f32, 1.0, 16.0))
    ssd_d = 1.0 + normal(ks[12], (N_SSD, SSD_HEADS), 0.02)
    ssd_gate_norm = 1.0 + normal(ks[13], (N_SSD, D_INNER), 0.02)
    ssd_out_proj = normal(ks[14], (N_SSD, D_INNER, D_MODEL), D_INNER ** -0.5)
    final_norm = 1.0 + normal(ks[15], (D_MODEL,), 0.02)
    return {"x": x, "norm_w": norm_w,
            "gla_in_proj": gla_in_proj, "gla_gate_up": gla_gate_up, "gla_gate_bias": gla_gate_bias,
            "gla_head_norm": gla_head_norm, "gla_out_proj": gla_out_proj,
            "ssd_in_proj": ssd_in_proj, "ssd_conv_w": ssd_conv_w, "ssd_conv_b": ssd_conv_b,
            "ssd_dt_bias": ssd_dt_bias, "ssd_a_log": ssd_a_log, "ssd_d": ssd_d,
            "ssd_gate_norm": ssd_gate_norm, "ssd_out_proj": ssd_out_proj,
            "final_norm": final_norm}


def reference(x, norm_w, gla_in_proj, gla_gate_up, gla_gate_bias, gla_head_norm, gla_out_proj,
              ssd_in_proj, ssd_conv_w, ssd_conv_b, ssd_dt_bias, ssd_a_log, ssd_d,
              ssd_gate_norm, ssd_out_proj, final_norm):
    for i in range(DEPTH):
        hn = rmsnorm(x, norm_w[i]).astype(x.dtype)
        j = i // N_MIXERS
        if i % N_MIXERS == 0:
            y = gla_mixer(hn, gla_in_proj[j], gla_gate_up[j], gla_gate_bias[j], gla_head_norm[j],
                          gla_out_proj[j])
        else:
            y = ssd_mixer(hn, ssd_in_proj[j], ssd_conv_w[j], ssd_conv_b[j], ssd_dt_bias[j],
                          ssd_a_log[j], ssd_d[j], ssd_gate_norm[j], ssd_out_proj[j])
        x = x + y
    return rmsnorm(x, final_norm).astype(x.dtype)
```

```python
import functools

import numpy as np
import jax
import jax.numpy as jnp
from jax import lax
from jax.experimental import pallas as pl
from jax.experimental.pallas import tpu as pltpu

F32 = jnp.float32
BF16 = jnp.bfloat16

D_MODEL = 1024
D_INNER = 2048
RMS_EPS = 1e-6
CHUNK = 64

GLA_HEADS = 4
GLA_DK = 512
GLA_DV = 2048
GLA_HEAD_K = 128
GLA_HEAD_V = 512
GLA_GATE_RANK = 16
GLA_GATE_NORMALIZER = 16.0
GLA_LEVELS = (32, 16, 8, 4, 2, 1)

SSD_HEAD_DIM = 64
SSD_HEADS = 32
SSD_GROUPS = 8
SSD_STATE = 128
SSD_CONV = 4
SSD_CONV_DIM = 4096
SSD_GROUP_WIDTH = (SSD_HEADS // SSD_GROUPS) * SSD_HEAD_DIM

LANES = 128
CONV_HALO = 8
BLOCK_ROWS = 256
VMEM_LIMIT_BYTES = 56 * 1024 * 1024


def _dot(a, b):
    return jnp.dot(a, b, preferred_element_type=F32)


def _dot_nt(a, b):
    return lax.dot_general(a, b, (((1,), (1,)), ((), ())), preferred_element_type=F32)


def _dot_tn(a, b):
    return lax.dot_general(a, b, (((0,), (0,)), ((), ())), preferred_element_type=F32)


def _split3(x):
    x1 = x.astype(BF16)
    r1 = x - x1.astype(F32)
    x2 = r1.astype(BF16)
    x3 = (r1 - x2.astype(F32)).astype(BF16)
    return x1, x2, x3


def _rms_scale(x):
    return lax.rsqrt(jnp.mean(x * x, axis=-1, keepdims=True) + RMS_EPS)


def _sigmoid(x):
    return 1.0 / (1.0 + jnp.exp(-x))


def _gla_constants():
    c = CHUNK
    i = np.arange(c)[:, None]
    m = np.arange(c)[None, :]
    sets = [m <= i, m > i]
    roles_q, roles_k, pair_masks = [], [], [(i == m)]
    for s in GLA_LEVELS:
        blk = i // s
        lower = (blk % 2) == 1
        prefix = (m >= blk * s) & (m <= i)
        suffix = (m > i) & (m < (blk + 1) * s)
        sets.append(np.where(lower, prefix, suffix))
        roles_q.append(np.broadcast_to(lower, (c, LANES)))
        roles_k.append(np.broadcast_to(~lower, (c, LANES)))
        pair_masks.append(lower & ((m // s) == blk - 1))
    sum_mat = np.concatenate(sets, axis=0).astype(np.float32)
    sum_mat = np.concatenate([sum_mat] * 3, axis=1)
    return (jnp.asarray(sum_mat, BF16),
            jnp.asarray(np.stack(roles_q), F32), jnp.asarray(np.stack(roles_k), F32),
            jnp.asarray(np.stack(pair_masks), F32))


def _ssd_constants():
    c = CHUNK
    i = np.arange(c)[:, None]
    m = np.arange(c)[None, :]
    tri = (m <= i).astype(np.float32)
    tri3 = np.concatenate([tri] * 3, axis=1)
    expand = np.zeros((LANES, D_INNER), np.float32)
    for part in range(3):
        for h in range(SSD_HEADS):
            expand[part * SSD_HEADS + h, h * SSD_HEAD_DIM:(h + 1) * SSD_HEAD_DIM] = 1.0
    col = np.arange(D_INNER)[None, :] % SSD_HEAD_DIM
    eye_tiled = (col == i).astype(np.float32)
    causal_tiled = (col <= i)
    causal_bias = np.where(causal_tiled, 0.0, -np.inf).astype(np.float32)
    return (jnp.asarray(tri3, BF16), jnp.asarray(expand, BF16),
            jnp.asarray(eye_tiled, F32), jnp.asarray(causal_bias, F32))


def _gla_kernel(x_ref, nw_ref, wqk_ref, wv_ref, wg_ref, wgk_ref, wgu_ref, bg_ref, hnw_ref, wout_ref,
                summat_ref, roleq_ref, rolek_ref, pmask_ref,
                out_ref,
                qk_s, v_s, g_s, a_s, o_s, state_s, *, block_rows):
    @pl.when(pl.program_id(0) == 0)
    def _():
        state_s[...] = jnp.zeros_like(state_s)

    x = x_ref[...]
    hn = (x * _rms_scale(x) * nw_ref[...]).astype(BF16)
    qk_s[...] = _dot(hn, wqk_ref[...])
    v_s[...] = _dot(hn, wv_ref[...]).astype(BF16)
    g_s[...] = _dot(hn, wg_ref[...])
    gk_low = _dot(hn, wgk_ref[...])
    logit = _dot(gk_low.astype(BF16), wgu_ref[...]) + bg_ref[...]
    log_sig = jnp.minimum(logit, 0.0) - jnp.log1p(jnp.exp(-jnp.abs(logit)))
    a_s[...] = log_sig * (1.0 / GLA_GATE_NORMALIZER)

    q_scale = GLA_HEAD_K ** -0.5

    def chunk_step(c, carry):
        rows = pl.ds(pl.multiple_of(c * CHUNK, CHUNK), CHUNK)
        a1, a2, a3 = _split3(a_s[rows, :])
        expo = _dot(summat_ref[...], jnp.concatenate([a1, a2, a3], axis=0))
        decay = jnp.exp(expo)
        for h in range(GLA_HEADS):
            kl = slice(h * GLA_HEAD_K, (h + 1) * GLA_HEAD_K)
            vl = slice(h * GLA_HEAD_V, (h + 1) * GLA_HEAD_V)
            qh = qk_s[rows, kl] * q_scale
            kh = qk_s[rows, GLA_DK + h * GLA_HEAD_K:GLA_DK + (h + 1) * GLA_HEAD_K]
            d_prefix = decay[0:CHUNK, kl]
            d_suffix = decay[CHUNK:2 * CHUNK, kl]
            att = pmask_ref[0] * _dot_nt(qh.astype(BF16), kh.astype(BF16))
            for lv in range(len(GLA_LEVELS)):
                d_lv = decay[(2 + lv) * CHUNK:(3 + lv) * CHUNK, kl]
                fq = (qh * (d_lv * roleq_ref[lv])).astype(BF16)
                fk = (kh * (d_lv * rolek_ref[lv])).astype(BF16)
                att = att + pmask_ref[lv + 1] * _dot_nt(fq, fk)
            state = state_s[h]
            vh = v_s[rows, vl]
            o = _dot((qh * d_prefix).astype(BF16), state.astype(BF16)) + _dot(att.astype(BF16), vh)
            d_last = jnp.broadcast_to(decay[CHUNK - 1:CHUNK, kl], (GLA_HEAD_K, GLA_HEAD_K)).T
            d_last = jnp.concatenate([d_last] * (GLA_HEAD_V // GLA_HEAD_K), axis=1)
            state_s[h] = state * d_last + _dot_tn((kh * d_suffix).astype(BF16), vh)
            on = o * _rms_scale(o) * hnw_ref[...]
            g = g_s[rows, vl]
            o_s[rows, vl] = (on * (g * _sigmoid(g))).astype(BF16)
        return carry

    lax.fori_loop(0, block_rows // CHUNK, chunk_step, 0)
    out_ref[...] = x + _dot(o_s[...], wout_ref[...])


def _const_spec(arr):
    nd = arr.ndim
    return pl.BlockSpec(arr.shape, lambda i, _nd=nd: (0,) * _nd, pipeline_mode=pl.Buffered(1))


def _gla_layer(x, norm_w, w_in, w_gate_up, b_gate_up, w_head_norm, w_out, *, block_rows):
    seq = x.shape[0]
    t = block_rows
    wqk = w_in[:, :2 * GLA_DK].astype(BF16)
    wv = w_in[:, 2 * GLA_DK:2 * GLA_DK + GLA_DV].astype(BF16)
    wg = w_in[:, 2 * GLA_DK + GLA_DV:2 * GLA_DK + 2 * GLA_DV].astype(BF16)
    wgk = jnp.pad(w_in[:, 2 * GLA_DK + 2 * GLA_DV:], ((0, 0), (0, LANES - GLA_GATE_RANK))).astype(BF16)
    wgu = jnp.pad(w_gate_up, ((0, LANES - GLA_GATE_RANK), (0, 0))).astype(BF16)
    consts = _gla_constants()
    operands = (norm_w.reshape(1, D_MODEL), wqk, wv, wg, wgk, wgu, b_gate_up.reshape(1, GLA_DK),
                w_head_norm.reshape(1, GLA_HEAD_V), w_out.astype(BF16)) + consts
    row_spec = pl.BlockSpec((t, D_MODEL), lambda i: (i, 0))
    return pl.pallas_call(
        functools.partial(_gla_kernel, block_rows=t),
        grid=(seq // t,),
        in_specs=[row_spec] + [_const_spec(a) for a in operands],
        out_specs=row_spec,
        out_shape=jax.ShapeDtypeStruct((seq, D_MODEL), F32),
        scratch_shapes=[
            pltpu.VMEM((t, 2 * GLA_DK), F32),
            pltpu.VMEM((t, GLA_DV), BF16),
            pltpu.VMEM((t, GLA_DV), F32),
            pltpu.VMEM((t, GLA_DK), F32),
            pltpu.VMEM((t, GLA_DV), BF16),
            pltpu.VMEM((GLA_HEADS, GLA_HEAD_K, GLA_HEAD_V), F32),
        ],
        compiler_params=pltpu.CompilerParams(dimension_semantics=("arbitrary",),
                                             vmem_limit_bytes=VMEM_LIMIT_BYTES),
        name="gla_layer",
    )(x, *operands)


def _ssd_kernel(x_ref, nw_ref, wz_ref, wxbc_ref, wdt_ref, convw_ref, convb_ref, dtb_ref, alog_ref,
                dskip_ref, gnw_ref, wout_ref, fnw_ref,
                tri3_ref, expand_ref, eye_ref, cbias_ref,
                out_ref,
                z_s, xbc_s, xs_s, bc_s, acol_s, dtx_s, y_s, state_s, *, block_rows):
    t = block_rows

    @pl.when(pl.program_id(0) == 0)
    def _():
        state_s[...] = jnp.zeros_like(state_s)
        xbc_s[0:CONV_HALO, :] = jnp.zeros((CONV_HALO, SSD_CONV_DIM), F32)

    x = x_ref[...]
    hn = (x * _rms_scale(x) * nw_ref[...]).astype(BF16)
    z_s[...] = _dot(hn, wz_ref[...])
    xbc_s[CONV_HALO:, :] = _dot(hn, wxbc_ref[...])
    dt_raw = _dot(hn, wdt_ref[...])

    conv = convb_ref[...] + convw_ref[SSD_CONV - 1:SSD_CONV, :] * xbc_s[CONV_HALO:, :]
    for w in range(SSD_CONV - 1):
        shift = SSD_CONV - 1 - w
        conv = conv + convw_ref[w:w + 1, :] * xbc_s[pl.ds(CONV_HALO - shift, t), :]
    xbc_s[0:CONV_HALO, :] = xbc_s[t:t + CONV_HALO, :]
    conv = conv * _sigmoid(conv)
    xs_s[...] = conv[:, :D_INNER]
    bc_s[...] = conv[:, D_INNER:].astype(BF16)

    dt_in = dt_raw + dtb_ref[...]
    dt = jnp.maximum(dt_in, 0.0) + jnp.log1p(jnp.exp(-jnp.abs(dt_in)))
    d_a = dt * (-jnp.exp(alog_ref[...]))
    lane = lax.broadcasted_iota(jnp.int32, (t, LANES), 1)

    def lane_groups(v):
        v1, v2, v3 = _split3(v)
        return jnp.where(lane < SSD_HEADS, v1, jnp.where(lane < 2 * SSD_HEADS, v2, v3))

    a_cum = []
    for c in range(t // CHUNK):
        a1, a2, a3 = _split3(d_a[c * CHUNK:(c + 1) * CHUNK])
        a_cum.append(_dot(tri3_ref[...], jnp.concatenate([a1, a2, a3], axis=0)))
    acol_s[...] = _dot(lane_groups(jnp.concatenate(a_cum, axis=0)), expand_ref[...])
    dtx_s[...] = _dot(lane_groups(dt), expand_ref[...])

    def chunk_step(c, carry):
        rows = pl.ds(pl.multiple_of(c * CHUNK, CHUNK), CHUNK)
        acol = acol_s[rows, :]
        xs = xs_s[rows, :]
        xdt = xs * dtx_s[rows, :]
        bm = bc_s[rows, 0:SSD_GROUPS * SSD_STATE]
        cm = bc_s[rows, SSD_GROUPS * SSD_STATE:]
        arow = jnp.sum(acol * eye_ref[...], axis=0, keepdims=True)
        lmat = jnp.exp(acol - arow + cbias_ref[...])
        a_last = acol[CHUNK - 1:CHUNK, :]
        from_start = jnp.exp(acol)
        to_end = jnp.exp(a_last - acol)
        xw = (xdt * to_end).astype(BF16)
        xdt_b = xdt.astype(BF16)
        chunk_decay = jnp.exp(a_last)
        lane2 = lax.broadcasted_iota(jnp.int32, (CHUNK, LANES), 1)
        y_parts = []
        for g in range(SSD_GROUPS):
            sl = slice(g * SSD_STATE, (g + 1) * SSD_STATE)
            gl = slice(g * SSD_GROUP_WIDTH, (g + 1) * SSD_GROUP_WIDTH)
            b_g = bm[:, sl]
            c_g = cm[:, sl]
            cb2 = _dot_nt(c_g, jnp.concatenate([b_g, b_g], axis=0))
            state = state_s[g]
            y_g = _dot(c_g, state.astype(BF16)) * from_start[:, gl]
            diag = []
            for half in range(SSD_GROUP_WIDTH // LANES):
                tl = slice(g * SSD_GROUP_WIDTH + half * LANES, g * SSD_GROUP_WIDTH + (half + 1) * LANES)
                m_pair = (cb2 * lmat[:, tl]).astype(BF16)
                x_pair = xdt_b[:, tl]
                zero = jnp.zeros_like(x_pair)
                rhs = jnp.concatenate([jnp.where(lane2 < SSD_HEAD_DIM, x_pair, zero),
                                       jnp.where(lane2 < SSD_HEAD_DIM, zero, x_pair)], axis=0)
                diag.append(_dot(m_pair, rhs))
            y_g = y_g + jnp.concatenate(diag, axis=1)
            state_s[g] = state * chunk_decay[:, gl] + _dot_tn(b_g, xw[:, gl])
            y_parts.append(y_g)
        y = jnp.concatenate(y_parts, axis=1) + xs * dskip_ref[...]
        z = z_s[rows, :]
        y = y * (z * _sigmoid(z))
        y_s[rows, :] = (y * _rms_scale(y) * gnw_ref[...]).astype(BF16)
        return carry

    lax.fori_loop(0, t // CHUNK, chunk_step, 0)
    res = x + _dot(y_s[...], wout_ref[...])
    out_ref[...] = res * _rms_scale(res) * fnw_ref[...]


def _ssd_layer(x, norm_w, w_in, conv_w, conv_b, dt_bias, a_log, d_skip, w_gate_norm, w_out, final_norm,
               *, block_rows):
    seq = x.shape[0]
    t = block_rows
    wz = w_in[:, :D_INNER].astype(BF16)
    wxbc = w_in[:, D_INNER:D_INNER + SSD_CONV_DIM].astype(BF16)
    w_dt = w_in[:, D_INNER + SSD_CONV_DIM:]
    pad = jnp.zeros((D_MODEL, LANES - 3 * SSD_HEADS), w_dt.dtype)
    wdt = jnp.concatenate([w_dt, w_dt, w_dt, pad], axis=1).astype(BF16)

    def head_lanes(v):
        return jnp.concatenate([v, v, v, jnp.zeros((LANES - 3 * SSD_HEADS,), v.dtype)]).reshape(1, LANES)

    consts = _ssd_constants()
    operands = (norm_w.reshape(1, D_MODEL), wz, wxbc, wdt, conv_w, conv_b.reshape(1, SSD_CONV_DIM),
                head_lanes(dt_bias), head_lanes(a_log),
                jnp.repeat(d_skip, SSD_HEAD_DIM).reshape(1, D_INNER),
                w_gate_norm.reshape(1, D_INNER), w_out.astype(BF16),
                final_norm.reshape(1, D_MODEL)) + consts
    row_spec = pl.BlockSpec((t, D_MODEL), lambda i: (i, 0))
    return pl.pallas_call(
        functools.partial(_ssd_kernel, block_rows=t),
        grid=(seq // t,),
        in_specs=[row_spec] + [_const_spec(a) for a in operands],
        out_specs=row_spec,
        out_shape=jax.ShapeDtypeStruct((seq, D_MODEL), F32),
        scratch_shapes=[
            pltpu.VMEM((t, D_INNER), F32),
            pltpu.VMEM((t + CONV_HALO, SSD_CONV_DIM), F32),
            pltpu.VMEM((t, D_INNER), F32),
            pltpu.VMEM((t, 2 * SSD_GROUPS * SSD_STATE), BF16),
            pltpu.VMEM((t, D_INNER), F32),
            pltpu.VMEM((t, D_INNER), F32),
            pltpu.VMEM((t, D_INNER), BF16),
            pltpu.VMEM((SSD_GROUPS, SSD_STATE, SSD_GROUP_WIDTH), F32),
        ],
        compiler_params=pltpu.CompilerParams(dimension_semantics=("arbitrary",),
                                             vmem_limit_bytes=VMEM_LIMIT_BYTES),
        name="ssd_layer",
    )(x, *operands)


def kernel(x, norm_w, gla_in_proj, gla_gate_up, gla_gate_bias, gla_head_norm, gla_out_proj, ssd_in_proj,
           ssd_conv_w, ssd_conv_b, ssd_dt_bias, ssd_a_log, ssd_d, ssd_gate_norm, ssd_out_proj, final_norm):
    bsz, seq, _ = x.shape
    assert bsz == 1 and seq % BLOCK_ROWS == 0
    h = x.reshape(seq, D_MODEL)
    h = _gla_layer(h, norm_w[0], gla_in_proj[0], gla_gate_up[0], gla_gate_bias[0], gla_head_norm[0],
                   gla_out_proj[0], block_rows=BLOCK_ROWS)
    h = _ssd_layer(h, norm_w[1], ssd_in_proj[0], ssd_conv_w[0], ssd_conv_b[0], ssd_dt_bias[0], ssd_a_log[0],
                   ssd_d[0], ssd_gate_norm[0], ssd_out_proj[0], final_norm, block_rows=BLOCK_ROWS)
    return h.reshape(bsz, seq, D_MODEL)
```

```python
import functools

import numpy as np
import jax
import jax.numpy as jnp
from jax import lax
from jax.experimental import pallas as pl
from jax.experimental.pallas import tpu as pltpu

F32 = jnp.float32
BF16 = jnp.bfloat16

D_MODEL = 1024
D_INNER = 2048
RMS_EPS = 1e-6
CHUNK = 64

GLA_HEADS = 4
GLA_DK = 512
GLA_DV = 2048
GLA_HEAD_K = 128
GLA_HEAD_V = 512
GLA_GATE_RANK = 16
GLA_GATE_NORMALIZER = 16.0
GLA_LEVELS = (32, 16, 8, 4, 2, 1)

SSD_HEAD_DIM = 64
SSD_HEADS = 32
SSD_GROUPS = 8
SSD_STATE = 128
SSD_CONV = 4
SSD_CONV_DIM = 4096
SSD_GROUP_WIDTH = (SSD_HEADS // SSD_GROUPS) * SSD_HEAD_DIM

LANES = 128
CONV_HALO = 8
BLOCK_ROWS = 256
VMEM_LIMIT_BYTES = 56 * 1024 * 1024


def _dot(a, b):
    return jnp.dot(a, b, preferred_element_type=F32)


def _dot_nt(a, b):
    return lax.dot_general(a, b, (((1,), (1,)), ((), ())), preferred_element_type=F32)


def _dot_tn(a, b):
    return lax.dot_general(a, b, (((0,), (0,)), ((), ())), preferred_element_type=F32)


def _split3(x):
    x1 = x.astype(BF16)
    r1 = x - x1.astype(F32)
    x2 = r1.astype(BF16)
    x3 = (r1 - x2.astype(F32)).astype(BF16)
    return x1, x2, x3


def _rms_scale(x):
    return lax.rsqrt(jnp.mean(x * x, axis=-1, keepdims=True) + RMS_EPS)


def _sigmoid(x):
    return 1.0 / (1.0 + jnp.exp(-x))


def _gla_constants():
    c = CHUNK
    i = np.arange(c)[:, None]
    m = np.arange(c)[None, :]
    sets = [m <= i, m > i]
    roles_q, roles_k, pair_masks = [], [], [(i == m)]
    for s in GLA_LEVELS:
        blk = i // s
        lower = (blk % 2) == 1
        prefix = (m >= blk * s) & (m <= i)
        suffix = (m > i) & (m < (blk + 1) * s)
        sets.append(np.where(lower, prefix, suffix))
        roles_q.append(np.broadcast_to(lower, (c, LANES)))
        roles_k.append(np.broadcast_to(~lower, (c, LANES)))
        pair_masks.append(lower & ((m // s) == blk - 1))
    sum_mat = np.concatenate(sets, axis=0).astype(np.float32)
    sum_mat = np.concatenate([sum_mat] * 3, axis=1)
    return (jnp.asarray(sum_mat, BF16),
            jnp.asarray(np.stack(roles_q), F32), jnp.asarray(np.stack(roles_k), F32),
            jnp.asarray(np.stack(pair_masks), F32))


def _ssd_constants():
    c = CHUNK
    i = np.arange(c)[:, None]
    m = np.arange(c)[None, :]
    tri = (m <= i).astype(np.float32)
    tri3 = np.concatenate([tri] * 3, axis=1)
    expand = np.zeros((LANES, D_INNER), np.float32)
    for part in range(3):
        for h in range(SSD_HEADS):
            expand[part * SSD_HEADS + h, h * SSD_HEAD_DIM:(h + 1) * SSD_HEAD_DIM] = 1.0
    col = np.arange(D_INNER)[None, :] % SSD_HEAD_DIM
    eye_tiled = (col == i).astype(np.float32)
    causal_tiled = (col <= i)
    causal_bias = np.where(causal_tiled, 0.0, -np.inf).astype(np.float32)
    return (jnp.asarray(tri3, BF16), jnp.asarray(expand, BF16),
            jnp.asarray(eye_tiled, F32), jnp.asarray(causal_bias, F32))


def _gla_kernel(x_ref, nw_ref, wqk_ref, wv_ref, wg_ref, wgk_ref, wgu_ref, bg_ref, hnw_ref, wout_ref,
                summat_ref, roleq_ref, rolek_ref, pmask_ref,
                out_ref,
                qk_s, v_s, g_s, a_s, o_s, state_s, *, block_rows):
    @pl.when(pl.program_id(0) == 0)
    def _():
        state_s[...] = jnp.zeros_like(state_s)

    x = x_ref[...]
    hn = (x * _rms_scale(x) * nw_ref[...]).astype(BF16)
    qk_s[...] = _dot(hn, wqk_ref[...])
    v_s[...] = _dot(hn, wv_ref[...]).astype(BF16)
    g_s[...] = _dot(hn, wg_ref[...])
    gk_low = _dot(hn, wgk_ref[...])
    logit = _dot(gk_low.astype(BF16), wgu_ref[...]) + bg_ref[...]
    log_sig = jnp.minimum(logit, 0.0) - jnp.log1p(jnp.exp(-jnp.abs(logit)))
    a_s[...] = log_sig * (1.0 / GLA_GATE_NORMALIZER)

    q_scale = GLA_HEAD_K ** -0.5

    def chunk_step(c, carry):
        rows = slice(c * CHUNK, (c + 1) * CHUNK)
        a1, a2, a3 = _split3(a_s[rows, :])
        expo = _dot(summat_ref[...], jnp.concatenate([a1, a2, a3], axis=0))
        decay = jnp.exp(expo)
        for h in range(GLA_HEADS):
            kl = slice(h * GLA_HEAD_K, (h + 1) * GLA_HEAD_K)
            vl = slice(h * GLA_HEAD_V, (h + 1) * GLA_HEAD_V)
            qh = qk_s[rows, kl] * q_scale
            kh = qk_s[rows, GLA_DK + h * GLA_HEAD_K:GLA_DK + (h + 1) * GLA_HEAD_K]
            d_prefix = decay[0:CHUNK, kl]
            d_suffix = decay[CHUNK:2 * CHUNK, kl]
            att = pmask_ref[0] * _dot_nt(qh.astype(BF16), kh.astype(BF16))
            for lv in range(len(GLA_LEVELS)):
                d_lv = decay[(2 + lv) * CHUNK:(3 + lv) * CHUNK, kl]
                fq = (qh * (d_lv * roleq_ref[lv])).astype(BF16)
                fk = (kh * (d_lv * rolek_ref[lv])).astype(BF16)
                att = att + pmask_ref[lv + 1] * _dot_nt(fq, fk)
            state = state_s[h]
            vh = v_s[rows, vl]
            o = _dot((qh * d_prefix).astype(BF16), state.astype(BF16)) + _dot(att.astype(BF16), vh)
            d_last = jnp.broadcast_to(decay[CHUNK - 1:CHUNK, kl], (GLA_HEAD_K, GLA_HEAD_K)).T
            d_last = jnp.concatenate([d_last] * (GLA_HEAD_V // GLA_HEAD_K), axis=1)
            state_s[h] = state * d_last + _dot_tn((kh * d_suffix).astype(BF16), vh)
            on = o * _rms_scale(o) * hnw_ref[...]
            g = g_s[rows, vl]
            o_s[rows, vl] = (on * (g * _sigmoid(g))).astype(BF16)
        return carry

    for c in range(block_rows // CHUNK):
        chunk_step(c, 0)
    out_ref[...] = x + _dot(o_s[...], wout_ref[...])


def _const_spec(arr):
    nd = arr.ndim
    return pl.BlockSpec(arr.shape, lambda i, _nd=nd: (0,) * _nd, pipeline_mode=pl.Buffered(1))


def _gla_layer(x, norm_w, w_in, w_gate_up, b_gate_up, w_head_norm, w_out, *, block_rows):
    seq = x.shape[0]
    t = block_rows
    wqk = w_in[:, :2 * GLA_DK].astype(BF16)
    wv = w_in[:, 2 * GLA_DK:2 * GLA_DK + GLA_DV].astype(BF16)
    wg = w_in[:, 2 * GLA_DK + GLA_DV:2 * GLA_DK + 2 * GLA_DV].astype(BF16)
    wgk = jnp.pad(w_in[:, 2 * GLA_DK + 2 * GLA_DV:], ((0, 0), (0, LANES - GLA_GATE_RANK))).astype(BF16)
    wgu = jnp.pad(w_gate_up, ((0, LANES - GLA_GATE_RANK), (0, 0))).astype(BF16)
    consts = _gla_constants()
    operands = (norm_w.reshape(1, D_MODEL), wqk, wv, wg, wgk, wgu, b_gate_up.reshape(1, GLA_DK),
                w_head_norm.reshape(1, GLA_HEAD_V), w_out.astype(BF16)) + consts
    row_spec = pl.BlockSpec((t, D_MODEL), lambda i: (i, 0))
    return pl.pallas_call(
        functools.partial(_gla_kernel, block_rows=t),
        grid=(seq // t,),
        in_specs=[row_spec] + [_const_spec(a) for a in operands],
        out_specs=row_spec,
        out_shape=jax.ShapeDtypeStruct((seq, D_MODEL), F32),
        scratch_shapes=[
            pltpu.VMEM((t, 2 * GLA_DK), F32),
            pltpu.VMEM((t, GLA_DV), BF16),
            pltpu.VMEM((t, GLA_DV), F32),
            pltpu.VMEM((t, GLA_DK), F32),
            pltpu.VMEM((t, GLA_DV), BF16),
            pltpu.VMEM((GLA_HEADS, GLA_HEAD_K, GLA_HEAD_V), F32),
        ],
        compiler_params=pltpu.CompilerParams(dimension_semantics=("arbitrary",),
                                             vmem_limit_bytes=VMEM_LIMIT_BYTES),
        name="gla_layer",
    )(x, *operands)


def _ssd_kernel(x_ref, nw_ref, wz_ref, wxbc_ref, wdt_ref, convw_ref, convb_ref, dtb_ref, alog_ref,
                dskip_ref, gnw_ref, wout_ref, fnw_ref,
                tri3_ref, expand_ref, eye_ref, cbias_ref,
                out_ref,
                z_s, xbc_s, xs_s, bc_s, acol_s, dtx_s, y_s, state_s, *, block_rows):
    t = block_rows

    @pl.when(pl.program_id(0) == 0)
    def _():
        state_s[...] = jnp.zeros_like(state_s)
        xbc_s[0:CONV_HALO, :] = jnp.zeros((CONV_HALO, SSD_CONV_DIM), F32)

    x = x_ref[...]
    hn = (x * _rms_scale(x) * nw_ref[...]).astype(BF16)
    z_s[...] = _dot(hn, wz_ref[...])
    xbc_s[CONV_HALO:, :] = _dot(hn, wxbc_ref[...])
    dt_raw = _dot(hn, wdt_ref[...])

    conv = convb_ref[...] + convw_ref[SSD_CONV - 1:SSD_CONV, :] * xbc_s[CONV_HALO:, :]
    for w in range(SSD_CONV - 1):
        shift = SSD_CONV - 1 - w
        conv = conv + convw_ref[w:w + 1, :] * xbc_s[pl.ds(CONV_HALO - shift, t), :]
    xbc_s[0:CONV_HALO, :] = xbc_s[t:t + CONV_HALO, :]
    conv = conv * _sigmoid(conv)
    xs_s[...] = conv[:, :D_INNER]
    bc_s[...] = conv[:, D_INNER:].astype(BF16)

    dt_in = dt_raw + dtb_ref[...]
    dt = jnp.maximum(dt_in, 0.0) + jnp.log1p(jnp.exp(-jnp.abs(dt_in)))
    d_a = dt * (-jnp.exp(alog_ref[...]))
    lane = lax.broadcasted_iota(jnp.int32, (t, LANES), 1)

    def lane_groups(v):
        v1, v2, v3 = _split3(v)
        return jnp.where(lane < SSD_HEADS, v1, jnp.where(lane < 2 * SSD_HEADS, v2, v3))

    a_cum = []
    for c in range(t // CHUNK):
        a1, a2, a3 = _split3(d_a[c * CHUNK:(c + 1) * CHUNK])
        a_cum.append(_dot(tri3_ref[...], jnp.concatenate([a1, a2, a3], axis=0)))
    acol_s[...] = _dot(lane_groups(jnp.concatenate(a_cum, axis=0)), expand_ref[...])
    dtx_s[...] = _dot(lane_groups(dt), expand_ref[...])

    def chunk_step(c, carry):
        rows = slice(c * CHUNK, (c + 1) * CHUNK)
        acol = acol_s[rows, :]
        xs = xs_s[rows, :]
        xdt = xs * dtx_s[rows, :]
        bm = bc_s[rows, 0:SSD_GROUPS * SSD_STATE]
        cm = bc_s[rows, SSD_GROUPS * SSD_STATE:]
        arow = jnp.sum(acol * eye_ref[...], axis=0, keepdims=True)
        lmat = jnp.exp(acol - arow + cbias_ref[...])
        a_last = acol[CHUNK - 1:CHUNK, :]
        from_start = jnp.exp(acol)
        to_end = jnp.exp(a_last - acol)
        xw = (xdt * to_end).astype(BF16)
        xdt_b = xdt.astype(BF16)
        chunk_decay = jnp.exp(a_last)
        lane2 = lax.broadcasted_iota(jnp.int32, (CHUNK, LANES), 1)
        y_parts = []
        for g in range(SSD_GROUPS):
            sl = slice(g * SSD_STATE, (g + 1) * SSD_STATE)
            gl = slice(g * SSD_GROUP_WIDTH, (g + 1) * SSD_GROUP_WIDTH)
            b_g = bm[:, sl]
            c_g = cm[:, sl]
            cb2 = _dot_nt(c_g, jnp.concatenate([b_g, b_g], axis=0))
            state = state_s[g]
            y_g = _dot(c_g, state.astype(BF16)) * from_start[:, gl]
            diag = []
            for half in range(SSD_GROUP_WIDTH // LANES):
                tl = slice(g * SSD_GROUP_WIDTH + half * LANES, g * SSD_GROUP_WIDTH + (half + 1) * LANES)
                m_pair = (cb2 * lmat[:, tl]).astype(BF16)
                x_pair = xdt_b[:, tl]
                zero = jnp.zeros_like(x_pair)
                rhs = jnp.concatenate([jnp.where(lane2 < SSD_HEAD_DIM, x_pair, zero),
                                       jnp.where(lane2 < SSD_HEAD_DIM, zero, x_pair)], axis=0)
                diag.append(_dot(m_pair, rhs))
            y_g = y_g + jnp.concatenate(diag, axis=1)
            state_s[g] = state * chunk_decay[:, gl] + _dot_tn(b_g, xw[:, gl])
            y_parts.append(y_g)
        y = jnp.concatenate(y_parts, axis=1) + xs * dskip_ref[...]
        z = z_s[rows, :]
        y = y * (z * _sigmoid(z))
        y_s[rows, :] = (y * _rms_scale(y) * gnw_ref[...]).astype(BF16)
        return carry

    for c in range(t // CHUNK):
        chunk_step(c, 0)
    res = x + _dot(y_s[...], wout_ref[...])
    out_ref[...] = res * _rms_scale(res) * fnw_ref[...]


def _ssd_layer(x, norm_w, w_in, conv_w, conv_b, dt_bias, a_log, d_skip, w_gate_norm, w_out, final_norm,
               *, block_rows):
    seq = x.shape[0]
    t = block_rows
    wz = w_in[:, :D_INNER].astype(BF16)
    wxbc = w_in[:, D_INNER:D_INNER + SSD_CONV_DIM].astype(BF16)
    w_dt = w_in[:, D_INNER + SSD_CONV_DIM:]
    pad = jnp.zeros((D_MODEL, LANES - 3 * SSD_HEADS), w_dt.dtype)
    wdt = jnp.concatenate([w_dt, w_dt, w_dt, pad], axis=1).astype(BF16)

    def head_lanes(v):
        return jnp.concatenate([v, v, v, jnp.zeros((LANES - 3 * SSD_HEADS,), v.dtype)]).reshape(1, LANES)

    consts = _ssd_constants()
    operands = (norm_w.reshape(1, D_MODEL), wz, wxbc, wdt, conv_w, conv_b.reshape(1, SSD_CONV_DIM),
                head_lanes(dt_bias), head_lanes(a_log),
                jnp.repeat(d_skip, SSD_HEAD_DIM).reshape(1, D_INNER),
                w_gate_norm.reshape(1, D_INNER), w_out.astype(BF16),
                final_norm.reshape(1, D_MODEL)) + consts
    row_spec = pl.BlockSpec((t, D_MODEL), lambda i: (i, 0))
    return pl.pallas_call(
        functools.partial(_ssd_kernel, block_rows=t),
        grid=(seq // t,),
        in_specs=[row_spec] + [_const_spec(a) for a in operands],
        out_specs=row_spec,
        out_shape=jax.ShapeDtypeStruct((seq, D_MODEL), F32),
        scratch_shapes=[
            pltpu.VMEM((t, D_INNER), F32),
            pltpu.VMEM((t + CONV_HALO, SSD_CONV_DIM), F32),
            pltpu.VMEM((t, D_INNER), F32),
            pltpu.VMEM((t, 2 * SSD_GROUPS * SSD_STATE), BF16),
            pltpu.VMEM((t, D_INNER), F32),
            pltpu.VMEM((t, D_INNER), F32),
            pltpu.VMEM((t, D_INNER), BF16),
            pltpu.VMEM((SSD_GROUPS, SSD_STATE, SSD_GROUP_WIDTH), F32),
        ],
        compiler_params=pltpu.CompilerParams(dimension_semantics=("arbitrary",),
                                             vmem_limit_bytes=VMEM_LIMIT_BYTES),
        name="ssd_layer",
    )(x, *operands)


def kernel(x, norm_w, gla_in_proj, gla_gate_up, gla_gate_bias, gla_head_norm, gla_out_proj, ssd_in_proj,
           ssd_conv_w, ssd_conv_b, ssd_dt_bias, ssd_a_log, ssd_d, ssd_gate_norm, ssd_out_proj, final_norm):
    bsz, seq, _ = x.shape
    assert bsz == 1 and seq % BLOCK_ROWS == 0
    h = x.reshape(seq, D_MODEL)
    h = _gla_layer(h, norm_w[0], gla_in_proj[0], gla_gate_up[0], gla_gate_bias[0], gla_head_norm[0],
                   gla_out_proj[0], block_rows=BLOCK_ROWS)
    h = _ssd_layer(h, norm_w[1], ssd_in_proj[0], ssd_conv_w[0], ssd_conv_b[0], ssd_dt_bias[0], ssd_a_log[0],
                   ssd_d[0], ssd_gate_norm[0], ssd_out_proj[0], final_norm, block_rows=BLOCK_ROWS)
    return h.reshape(bsz, seq, D_MODEL)
```

```python
import functools

import numpy as np
import jax
import jax.numpy as jnp
from jax import lax
from jax.experimental import pallas as pl
from jax.experimental.pallas import tpu as pltpu

F32 = jnp.float32
BF16 = jnp.bfloat16

D_MODEL = 1024
D_INNER = 2048
RMS_EPS = 1e-6
CHUNK = 64

GLA_HEADS = 4
GLA_DK = 512
GLA_DV = 2048
GLA_HEAD_K = 128
GLA_HEAD_V = 512
GLA_GATE_RANK = 16
GLA_GATE_NORMALIZER = 16.0
GLA_LEVELS = (32, 16, 8, 4, 2, 1)

SSD_HEAD_DIM = 64
SSD_HEADS = 32
SSD_GROUPS = 8
SSD_STATE = 128
SSD_CONV = 4
SSD_CONV_DIM = 4096
SSD_GROUP_WIDTH = (SSD_HEADS // SSD_GROUPS) * SSD_HEAD_DIM

LANES = 128
CONV_HALO = 8
BLOCK_ROWS = 256
PIECE_COLS = 256
EXPAND_COLS = 512
PIPE_LAG = 1
VMEM_LIMIT_BYTES = 60 * 1024 * 1024


def _dot(a, b):
    return jnp.dot(a, b, preferred_element_type=F32)


def _dot_nt(a, b):
    return lax.dot_general(a, b, (((1,), (1,)), ((), ())), preferred_element_type=F32)


def _dot_tn(a, b):
    return lax.dot_general(a, b, (((0,), (0,)), ((), ())), preferred_element_type=F32)


def _split3(x):
    x1 = x.astype(BF16)
    r1 = x - x1.astype(F32)
    x2 = r1.astype(BF16)
    x3 = (r1 - x2.astype(F32)).astype(BF16)
    return x1, x2, x3


def _rms_scale(x):
    return lax.rsqrt(jnp.mean(x * x, axis=-1, keepdims=True) + RMS_EPS)


def _sigmoid(x):
    return 1.0 / (1.0 + jnp.exp(-x))


def _piece_scheduler(pieces, n_calls):
    total = sum(w for w, _ in pieces)
    state = {"call": 0, "next": 0, "done": 0}

    def run():
        state["call"] += 1
        target = total * state["call"] / n_calls
        while state["next"] < len(pieces) and (state["done"] < target or state["call"] == n_calls):
            w, fn = pieces[state["next"]]
            fn()
            state["done"] += w
            state["next"] += 1
    return run


def _run_pipeline(tasks, run_pieces, scores_fn, output_fn):
    n_slots = len(tasks) + PIPE_LAG
    scores = {}
    for slot in range(n_slots):
        run_pieces()
        if slot < len(tasks):
            scores[slot] = scores_fn(*tasks[slot])
        run_pieces()
        if slot >= PIPE_LAG:
            output_fn(*tasks[slot - PIPE_LAG], scores.pop(slot - PIPE_LAG))


def _by_parity(step_fn, set_a, set_b):
    step_id = pl.program_id(0)

    @pl.when(step_id % 2 == 0)
    def _():
        step_fn(set_b, set_a)

    @pl.when(step_id % 2 == 1)
    def _():
        step_fn(set_a, set_b)


def _gla_constants():
    c = CHUNK
    i = np.arange(c)[:, None]
    m = np.arange(c)[None, :]
    sets = [m <= i, m > i]
    roles_q, roles_k, pair_masks = [], [], [(i == m)]
    for s in GLA_LEVELS:
        blk = i // s
        lower = (blk % 2) == 1
        prefix = (m >= blk * s) & (m <= i)
        suffix = (m > i) & (m < (blk + 1) * s)
        sets.append(np.where(lower, prefix, suffix))
        roles_q.append(np.broadcast_to(lower, (c, LANES)))
        roles_k.append(np.broadcast_to(~lower, (c, LANES)))
        pair_masks.append(lower & ((m // s) == blk - 1))
    sum_mat = np.concatenate(sets, axis=0).astype(np.float32)
    sum_mat = np.concatenate([sum_mat] * 3, axis=1)
    return (jnp.asarray(sum_mat, BF16),
            jnp.asarray(np.stack(roles_q), F32), jnp.asarray(np.stack(roles_k), F32),
            jnp.asarray(np.stack(pair_masks), F32))


def _ssd_constants():
    c = CHUNK
    i = np.arange(c)[:, None]
    m = np.arange(c)[None, :]
    tri = (m <= i).astype(np.float32)
    tri3 = np.concatenate([tri] * 3, axis=1)
    expand = np.zeros((LANES, D_INNER), np.float32)
    for part in range(3):
        for h in range(SSD_HEADS):
            expand[part * SSD_HEADS + h, h * SSD_HEAD_DIM:(h + 1) * SSD_HEAD_DIM] = 1.0
    col = np.arange(D_INNER)[None, :] % SSD_HEAD_DIM
    eye_tiled = (col == i).astype(np.float32)
    causal_tiled = (col <= i)
    causal_bias = np.where(causal_tiled, 0.0, -np.inf).astype(np.float32)
    return (jnp.asarray(tri3, BF16), jnp.asarray(expand, BF16),
            jnp.asarray(eye_tiled, F32), jnp.asarray(causal_bias, F32))


def _gla_kernel(x_ref, xold_ref, nw_ref, wqk_ref, wv_ref, wg_ref, wgk_ref, wgu_ref, bg_ref, hnw_ref, wout_ref,
                summat_ref, roleq_ref, rolek_ref, pmask_ref,
                out_ref,
                qk_a, v_a, g_a, a_a, o_a, qk_b, v_b, g_b, a_b, o_b, hn_s, gk_s, state_s, *, block_rows):
    set_a = (qk_a, v_a, g_a, a_a, o_a)
    set_b = (qk_b, v_b, g_b, a_b, o_b)

    @pl.when(pl.program_id(0) == 0)
    def _():
        state_s[...] = jnp.zeros_like(state_s)
        for buf in set_b:
            buf[...] = jnp.zeros_like(buf)

    def step(read_set, write_set):
        qk_s, v_s, g_s, a_s, o_old = read_set
        qk_n, v_n, g_n, a_n, o_s = write_set
        q_scale = GLA_HEAD_K ** -0.5
        n_chunks = block_rows // CHUNK
        decays = {}

        def chunk_decays(c):
            rows = slice(c * CHUNK, (c + 1) * CHUNK)
            a1, a2, a3 = _split3(a_s[rows, :])
            expo = _dot(summat_ref[...], jnp.concatenate([a1, a2, a3], axis=0))
            decays[c] = jnp.exp(expo)

        def head_scores(c, h):
            if h == 0 and c + 1 < n_chunks:
                chunk_decays(c + 1)
            rows = slice(c * CHUNK, (c + 1) * CHUNK)
            kl = slice(h * GLA_HEAD_K, (h + 1) * GLA_HEAD_K)
            decay = decays[c]
            qh = qk_s[rows, kl] * q_scale
            kh = qk_s[rows, GLA_DK + h * GLA_HEAD_K:GLA_DK + (h + 1) * GLA_HEAD_K]
            att = pmask_ref[0] * _dot_nt(qh.astype(BF16), kh.astype(BF16))
            for lv in range(len(GLA_LEVELS)):
                d_lv = decay[(2 + lv) * CHUNK:(3 + lv) * CHUNK, kl]
                fq = (qh * (d_lv * roleq_ref[lv])).astype(BF16)
                fk = (kh * (d_lv * rolek_ref[lv])).astype(BF16)
                att = att + pmask_ref[lv + 1] * _dot_nt(fq, fk)
            q_in = (qh * decay[0:CHUNK, kl]).astype(BF16)
            k_out = (kh * decay[CHUNK:2 * CHUNK, kl]).astype(BF16)
            return att.astype(BF16), q_in, k_out

        def head_output(c, h, scores):
            att, q_in, k_out = scores
            rows = slice(c * CHUNK, (c + 1) * CHUNK)
            kl = slice(h * GLA_HEAD_K, (h + 1) * GLA_HEAD_K)
            vl = slice(h * GLA_HEAD_V, (h + 1) * GLA_HEAD_V)
            state = state_s[h]
            vh = v_s[rows, vl]
            o = _dot(q_in, state.astype(BF16)) + _dot(att, vh)
            d_last = jnp.broadcast_to(decays[c][CHUNK - 1:CHUNK, kl], (GLA_HEAD_K, GLA_HEAD_K)).T
            d_last = jnp.concatenate([d_last] * (GLA_HEAD_V // GLA_HEAD_K), axis=1)
            state_s[h] = state * d_last + _dot_tn(k_out, vh)
            on = o * _rms_scale(o) * hnw_ref[...]
            g = g_s[rows, vl]
            o_s[rows, vl] = (on * (g * _sigmoid(g))).astype(BF16)

        x = x_ref[...]
        hn_s[...] = (x * _rms_scale(x) * nw_ref[...]).astype(BF16)

        def proj_piece(dst, w_ref, j):
            cols = slice(j * PIECE_COLS, (j + 1) * PIECE_COLS)
            def run():
                dst[:, cols] = _dot(hn_s[...], w_ref[:, cols]).astype(dst.dtype)
            return run

        def gate_low_piece():
            gk_s[...] = _dot(hn_s[...], wgk_ref[...]).astype(BF16)

        def gate_piece():
            logit = _dot(gk_s[...], wgu_ref[...]) + bg_ref[...]
            log_sig = jnp.minimum(logit, 0.0) - jnp.log1p(jnp.exp(-jnp.abs(logit)))
            a_n[...] = log_sig * (1.0 / GLA_GATE_NORMALIZER)

        def out_piece(j):
            cols = slice(j * PIECE_COLS, (j + 1) * PIECE_COLS)
            def run():
                out_ref[:, cols] = xold_ref[:, cols] + _dot(o_old[...], wout_ref[:, cols])
            return run

        pieces = ([(2, out_piece(j)) for j in range(D_MODEL // PIECE_COLS)]
                  + [(0.25, gate_low_piece)]
                  + [(1, proj_piece(qk_n, wqk_ref, j)) for j in range(2 * GLA_DK // PIECE_COLS)]
                  + [(0.5, gate_piece)]
                  + [(1, proj_piece(v_n, wv_ref, j)) for j in range(GLA_DV // PIECE_COLS)]
                  + [(1, proj_piece(g_n, wg_ref, j)) for j in range(GLA_DV // PIECE_COLS)])
        tasks = [(c, h) for c in range(n_chunks) for h in range(GLA_HEADS)]
        chunk_decays(0)
        _run_pipeline(tasks, _piece_scheduler(pieces, 2 * (len(tasks) + PIPE_LAG)), head_scores, head_output)

    _by_parity(step, set_a, set_b)


def _const_spec(arr):
    nd = arr.ndim
    return pl.BlockSpec(arr.shape, lambda i, _nd=nd: (0,) * _nd, pipeline_mode=pl.Buffered(1))


def _pipelined_call(kernel_fn, name, x, operands, buffer_set, extra_scratch, block_rows):
    seq = x.shape[0]
    t = block_rows
    nblk = seq // t
    cur_spec = pl.BlockSpec((t, D_MODEL), lambda i: (jnp.minimum(i, nblk - 1), 0))
    old_spec = pl.BlockSpec((t, D_MODEL), lambda i: (jnp.maximum(i - 2, 0), 0))
    return pl.pallas_call(
        functools.partial(kernel_fn, block_rows=t),
        grid=(nblk + 2,),
        in_specs=[cur_spec, old_spec] + [_const_spec(a) for a in operands],
        out_specs=old_spec,
        out_shape=jax.ShapeDtypeStruct((seq, D_MODEL), F32),
        scratch_shapes=buffer_set + buffer_set + extra_scratch,
        compiler_params=pltpu.CompilerParams(dimension_semantics=("arbitrary",),
                                             vmem_limit_bytes=VMEM_LIMIT_BYTES),
        name=name,
    )(x, x, *operands)


def _gla_layer(x, norm_w, w_in, w_gate_up, b_gate_up, w_head_norm, w_out, *, block_rows):
    t = block_rows
    wqk = w_in[:, :2 * GLA_DK].astype(BF16)
    wv = w_in[:, 2 * GLA_DK:2 * GLA_DK + GLA_DV].astype(BF16)
    wg = w_in[:, 2 * GLA_DK + GLA_DV:2 * GLA_DK + 2 * GLA_DV].astype(BF16)
    wgk = jnp.pad(w_in[:, 2 * GLA_DK + 2 * GLA_DV:], ((0, 0), (0, LANES - GLA_GATE_RANK))).astype(BF16)
    wgu = jnp.pad(w_gate_up, ((0, LANES - GLA_GATE_RANK), (0, 0))).astype(BF16)
    operands = (norm_w.reshape(1, D_MODEL), wqk, wv, wg, wgk, wgu, b_gate_up.reshape(1, GLA_DK),
                w_head_norm.reshape(1, GLA_HEAD_V), w_out.astype(BF16)) + _gla_constants()
    buffer_set = [
        pltpu.VMEM((t, 2 * GLA_DK), F32),
        pltpu.VMEM((t, GLA_DV), BF16),
        pltpu.VMEM((t, GLA_DV), F32),
        pltpu.VMEM((t, GLA_DK), F32),
        pltpu.VMEM((t, GLA_DV), BF16),
    ]
    extra = [
        pltpu.VMEM((t, D_MODEL), BF16),
        pltpu.VMEM((t, LANES), BF16),
        pltpu.VMEM((GLA_HEADS, GLA_HEAD_K, GLA_HEAD_V), F32),
    ]
    return _pipelined_call(_gla_kernel, "gla_layer", x, operands, buffer_set, extra, t)


def _ssd_kernel(x_ref, xold_ref, nw_ref, wz_ref, wxbc_ref, wdt_ref, convw_ref, convb_ref, dtb_ref, alog_ref,
                dskip_ref, gnw_ref, wout_ref, fnw_ref,
                tri3_ref, expand_ref, eye_ref, cbias_ref,
                out_ref,
                z_a, xs_a, bc_a, acol_a, dtx_a, y_a, z_b, xs_b, bc_b, acol_b, dtx_b, y_b,
                hn_s, halo_s, da_s, heads_s, state_s, *, block_rows):
    t = block_rows
    set_a = (z_a, xs_a, bc_a, acol_a, dtx_a, y_a)
    set_b = (z_b, xs_b, bc_b, acol_b, dtx_b, y_b)

    @pl.when(pl.program_id(0) == 0)
    def _():
        state_s[...] = jnp.zeros_like(state_s)
        halo_s[...] = jnp.zeros_like(halo_s)
        for buf in set_b:
            buf[...] = jnp.zeros_like(buf)

    def step(read_set, write_set):
        z_s, xs_s, bc_s, acol_s, dtx_s, y_old = read_set
        z_n, xs_n, bc_n, acol_n, dtx_n, y_s = write_set
        n_chunks = t // CHUNK

        x = x_ref[...]
        hn_s[...] = (x * _rms_scale(x) * nw_ref[...]).astype(BF16)

        def z_piece(j):
            cols = slice(j * PIECE_COLS, (j + 1) * PIECE_COLS)
            def run():
                z_n[:, cols] = _dot(hn_s[...], wz_ref[:, cols])
            return run

        def conv_piece(j):
            cols = slice(j * PIECE_COLS, (j + 1) * PIECE_COLS)
            def run():
                raw = _dot(hn_s[...], wxbc_ref[:, cols])
                ext = jnp.concatenate([halo_s[:, cols], raw], axis=0)
                conv = convb_ref[:, cols] + convw_ref[SSD_CONV - 1:SSD_CONV, cols] * raw
                for w in range(SSD_CONV - 1):
                    shifted = pltpu.roll(ext, SSD_CONV - 1 - w, 0)[CONV_HALO:, :]
                    conv = conv + convw_ref[w:w + 1, cols] * shifted
                halo_s[:, cols] = raw[t - CONV_HALO:, :]
                conv = conv * _sigmoid(conv)
                if j * PIECE_COLS < D_INNER:
                    xs_n[:, cols] = conv
                else:
                    bc_n[:, j * PIECE_COLS - D_INNER:(j + 1) * PIECE_COLS - D_INNER] = conv.astype(BF16)
            return run

        lane = lax.broadcasted_iota(jnp.int32, (t, LANES), 1)

        def lane_groups(v):
            v1, v2, v3 = _split3(v)
            return jnp.where(lane < SSD_HEADS, v1, jnp.where(lane < 2 * SSD_HEADS, v2, v3))

        def dt_piece():
            dt_in = _dot(hn_s[...], wdt_ref[...]) + dtb_ref[...]
            dt = jnp.maximum(dt_in, 0.0) + jnp.log1p(jnp.exp(-jnp.abs(dt_in)))
            heads_s[1] = lane_groups(dt)
            da_s[...] = dt * (-jnp.exp(alog_ref[...]))

        def cumsum_piece():
            a_cum = []
            for c in range(n_chunks):
                a1, a2, a3 = _split3(da_s[c * CHUNK:(c + 1) * CHUNK, :])
                a_cum.append(_dot(tri3_ref[...], jnp.concatenate([a1, a2, a3], axis=0)))
            heads_s[0] = lane_groups(jnp.concatenate(a_cum, axis=0))

        def expand_piece(dst, src, j):
            cols = slice(j * EXPAND_COLS, (j + 1) * EXPAND_COLS)
            def run():
                dst[:, cols] = _dot(heads_s[src], expand_ref[:, cols])
            return run

        def out_piece(j):
            cols = slice(j * PIECE_COLS, (j + 1) * PIECE_COLS)
            def run():
                out_ref[:, cols] = xold_ref[:, cols] + _dot(y_old[...], wout_ref[:, cols])
            return run

        def final_norm_piece():
            res = out_ref[...]
            out_ref[...] = res * _rms_scale(res) * fnw_ref[...]

        n_exp = D_INNER // EXPAND_COLS
        conv_pieces = [(1, conv_piece(j)) for j in range(SSD_CONV_DIM // PIECE_COLS)]
        exp_pieces = ([(0.5, expand_piece(acol_n, 0, j)) for j in range(n_exp)]
                      + [(0.5, expand_piece(dtx_n, 1, j)) for j in range(n_exp)])
        mixed = []
        for k in range(len(exp_pieces)):
            mixed += conv_pieces[2 * k:2 * k + 2] + [exp_pieces[k]]
        z_pieces = [(1, z_piece(j)) for j in range(D_INNER // PIECE_COLS)]
        pieces = ([(2, out_piece(j)) for j in range(D_MODEL // PIECE_COLS)]
                  + [(0.25, final_norm_piece), (0.25, dt_piece)]
                  + z_pieces[:4] + [(0.25, cumsum_piece)] + z_pieces[4:] + mixed)

        prep = {}
        lane2 = lax.broadcasted_iota(jnp.int32, (CHUNK, LANES), 1)

        def chunk_prep(c):
            rows = slice(c * CHUNK, (c + 1) * CHUNK)
            acol = acol_s[rows, :]
            xs = xs_s[rows, :]
            xdt = xs * dtx_s[rows, :]
            arow = jnp.sum(acol * eye_ref[...], axis=0, keepdims=True)
            a_last = acol[CHUNK - 1:CHUNK, :]
            prep[c] = dict(
                xs=xs,
                lmat=jnp.exp(acol - arow + cbias_ref[...]),
                from_start=jnp.exp(acol),
                xw=(xdt * jnp.exp(a_last - acol)).astype(BF16),
                xdt_b=xdt.astype(BF16),
                chunk_decay=jnp.exp(a_last),
                y=[])

        def group_scores(c, g):
            if g == SSD_GROUPS // 2 and c + 1 < n_chunks:
                chunk_prep(c + 1)
            rows = slice(c * CHUNK, (c + 1) * CHUNK)
            b_g = bc_s[rows, g * SSD_STATE:(g + 1) * SSD_STATE]
            c_g = bc_s[rows, (SSD_GROUPS + g) * SSD_STATE:(SSD_GROUPS + g + 1) * SSD_STATE]
            cb2 = _dot_nt(c_g, jnp.concatenate([b_g, b_g], axis=0))
            return b_g, c_g, cb2

        def group_output(c, g, scores):
            b_g, c_g, cb2 = scores
            p = prep[c]
            gl = slice(g * SSD_GROUP_WIDTH, (g + 1) * SSD_GROUP_WIDTH)
            state = state_s[g]
            y_g = _dot(c_g, state.astype(BF16)) * p["from_start"][:, gl]
            diag = []
            for half in range(SSD_GROUP_WIDTH // LANES):
                tl = slice(g * SSD_GROUP_WIDTH + half * LANES, g * SSD_GROUP_WIDTH + (half + 1) * LANES)
                m_pair = (cb2 * p["lmat"][:, tl]).astype(BF16)
                x_pair = p["xdt_b"][:, tl]
                zero = jnp.zeros_like(x_pair)
                rhs = jnp.concatenate([jnp.where(lane2 < SSD_HEAD_DIM, x_pair, zero),
                                       jnp.where(lane2 < SSD_HEAD_DIM, zero, x_pair)], axis=0)
                diag.append(_dot(m_pair, rhs))
            p["y"].append(y_g + jnp.concatenate(diag, axis=1))
            state_s[g] = state * p["chunk_decay"][:, gl] + _dot_tn(b_g, p["xw"][:, gl])
            if g == SSD_GROUPS - 1:
                rows = slice(c * CHUNK, (c + 1) * CHUNK)
                y = jnp.concatenate(p["y"], axis=1) + p["xs"] * dskip_ref[...]
                z = z_s[rows, :]
                y = y * (z * _sigmoid(z))
                y_s[rows, :] = (y * _rms_scale(y) * gnw_ref[...]).astype(BF16)
                del prep[c]

        tasks = [(c, g) for c in range(n_chunks) for g in range(SSD_GROUPS)]
        chunk_prep(0)
        _run_pipeline(tasks, _piece_scheduler(pieces, 2 * (len(tasks) + PIPE_LAG)), group_scores, group_output)

    _by_parity(step, set_a, set_b)


def _ssd_layer(x, norm_w, w_in, conv_w, conv_b, dt_bias, a_log, d_skip, w_gate_norm, w_out, final_norm,
               *, block_rows):
    t = block_rows
    wz = w_in[:, :D_INNER].astype(BF16)
    wxbc = w_in[:, D_INNER:D_INNER + SSD_CONV_DIM].astype(BF16)
    w_dt = w_in[:, D_INNER + SSD_CONV_DIM:]
    pad = jnp.zeros((D_MODEL, LANES - 3 * SSD_HEADS), w_dt.dtype)
    wdt = jnp.concatenate([w_dt, w_dt, w_dt, pad], axis=1).astype(BF16)

    def head_lanes(v):
        return jnp.concatenate([v, v, v, jnp.zeros((LANES - 3 * SSD_HEADS,), v.dtype)]).reshape(1, LANES)

    operands = (norm_w.reshape(1, D_MODEL), wz, wxbc, wdt, conv_w, conv_b.reshape(1, SSD_CONV_DIM),
                head_lanes(dt_bias), head_lanes(a_log),
                jnp.repeat(d_skip, SSD_HEAD_DIM).reshape(1, D_INNER),
                w_gate_norm.reshape(1, D_INNER), w_out.astype(BF16),
                final_norm.reshape(1, D_MODEL)) + _ssd_constants()
    buffer_set = [
        pltpu.VMEM((t, D_INNER), F32),
        pltpu.VMEM((t, D_INNER), F32),
        pltpu.VMEM((t, 2 * SSD_GROUPS * SSD_STATE), BF16),
        pltpu.VMEM((t, D_INNER), F32),
        pltpu.VMEM((t, D_INNER), F32),
        pltpu.VMEM((t, D_INNER), BF16),
    ]
    extra = [
        pltpu.VMEM((t, D_MODEL), BF16),
        pltpu.VMEM((CONV_HALO, SSD_CONV_DIM), F32),
        pltpu.VMEM((t, LANES), F32),
        pltpu.VMEM((2, t, LANES), BF16),
        pltpu.VMEM((SSD_GROUPS, SSD_STATE, SSD_GROUP_WIDTH), F32),
    ]
    return _pipelined_call(_ssd_kernel, "ssd_layer", x, operands, buffer_set, extra, t)


def kernel(x, norm_w, gla_in_proj, gla_gate_up, gla_gate_bias, gla_head_norm, gla_out_proj, ssd_in_proj,
           ssd_conv_w, ssd_conv_b, ssd_dt_bias, ssd_a_log, ssd_d, ssd_gate_norm, ssd_out_proj, final_norm):
    bsz, seq, _ = x.shape
    assert bsz == 1 and seq % BLOCK_ROWS == 0
    h = x.reshape(seq, D_MODEL)
    h = _gla_layer(h, norm_w[0], gla_in_proj[0], gla_gate_up[0], gla_gate_bias[0], gla_head_norm[0],
                   gla_out_proj[0], block_rows=BLOCK_ROWS)
    h = _ssd_layer(h, norm_w[1], ssd_in_proj[0], ssd_conv_w[0], ssd_conv_b[0], ssd_dt_bias[0], ssd_a_log[0],
                   ssd_d[0], ssd_gate_norm[0], ssd_out_proj[0], final_norm, block_rows=BLOCK_ROWS)
    return h.reshape(bsz, seq, D_MODEL)
```

```python
import functools

import numpy as np
import jax
import jax.numpy as jnp
from jax import lax
from jax.experimental import pallas as pl
from jax.experimental.pallas import tpu as pltpu

F32 = jnp.float32
BF16 = jnp.bfloat16

D_MODEL = 1024
D_INNER = 2048
RMS_EPS = 1e-6
CHUNK = 64

GLA_HEADS = 4
GLA_DK = 512
GLA_DV = 2048
GLA_HEAD_K = 128
GLA_HEAD_V = 512
GLA_GATE_RANK = 16
GLA_GATE_NORMALIZER = 16.0
GLA_LEVELS = (32, 16, 8, 4, 2, 1)

SSD_HEAD_DIM = 64
SSD_HEADS = 32
SSD_GROUPS = 8
SSD_STATE = 128
SSD_CONV = 4
SSD_CONV_DIM = 4096
SSD_GROUP_WIDTH = (SSD_HEADS // SSD_GROUPS) * SSD_HEAD_DIM

LANES = 128
CONV_HALO = 8
BLOCK_ROWS = 256
PIECE_COLS = 256
EXPAND_COLS = 512
PIPE_LAG = 1
VMEM_LIMIT_BYTES = 60 * 1024 * 1024


def _dot(a, b):
    return jnp.dot(a, b, preferred_element_type=F32)


def _dot_nt(a, b):
    return lax.dot_general(a, b, (((1,), (1,)), ((), ())), preferred_element_type=F32)


def _dot_tn(a, b):
    return lax.dot_general(a, b, (((0,), (0,)), ((), ())), preferred_element_type=F32)


def _split3(x):
    x1 = x.astype(BF16)
    r1 = x - x1.astype(F32)
    x2 = r1.astype(BF16)
    x3 = (r1 - x2.astype(F32)).astype(BF16)
    return x1, x2, x3


def _rms_scale(x):
    return lax.rsqrt(jnp.mean(x * x, axis=-1, keepdims=True) + RMS_EPS)


def _sigmoid(x):
    return 1.0 / (1.0 + jnp.exp(-x))


def _piece_scheduler(pieces, n_calls):
    total = sum(w for w, _ in pieces)
    state = {"call": 0, "next": 0, "done": 0}

    def run():
        state["call"] += 1
        target = total * state["call"] / n_calls
        while state["next"] < len(pieces) and (state["done"] < target or state["call"] == n_calls):
            w, fn = pieces[state["next"]]
            fn()
            state["done"] += w
            state["next"] += 1
    return run


def _merge_evenly(*streams):
    keyed = [((k + (0.5 if si else 0.0)) / len(s), si, k) for si, s in enumerate(streams) for k in range(len(s))]
    return [streams[si][k] for _, si, k in sorted(keyed)]


def _run_pipeline(tasks, run_pieces, scores_fn, output_fn):
    n_slots = len(tasks) + PIPE_LAG
    scores = {}
    for slot in range(n_slots):
        run_pieces()
        if slot < len(tasks):
            scores[slot] = scores_fn(*tasks[slot])
        run_pieces()
        if slot >= PIPE_LAG:
            output_fn(*tasks[slot - PIPE_LAG], scores.pop(slot - PIPE_LAG))


def _by_parity(step_fn, set_a, set_b):
    step_id = pl.program_id(0)

    @pl.when(step_id % 2 == 0)
    def _():
        step_fn(set_b, set_a)

    @pl.when(step_id % 2 == 1)
    def _():
        step_fn(set_a, set_b)


def _gla_constants():
    c = CHUNK
    i = np.arange(c)[:, None]
    m = np.arange(c)[None, :]
    sets = [m <= i, m > i]
    roles_q, roles_k, pair_masks = [], [], [(i == m)]
    for s in GLA_LEVELS:
        blk = i // s
        lower = (blk % 2) == 1
        prefix = (m >= blk * s) & (m <= i)
        suffix = (m > i) & (m < (blk + 1) * s)
        sets.append(np.where(lower, prefix, suffix))
        roles_q.append(np.broadcast_to(lower, (c, LANES)))
        roles_k.append(np.broadcast_to(~lower, (c, LANES)))
        pair_masks.append(lower & ((m // s) == blk - 1))
    sum_mat = np.concatenate(sets, axis=0).astype(np.float32)
    sum_mat = np.concatenate([sum_mat] * 3, axis=1)
    return (jnp.asarray(sum_mat, BF16),
            jnp.asarray(np.stack(roles_q), F32), jnp.asarray(np.stack(roles_k), F32),
            jnp.asarray(np.stack(pair_masks), F32))


def _ssd_constants():
    c = CHUNK
    i = np.arange(c)[:, None]
    m = np.arange(c)[None, :]
    tri = (m <= i).astype(np.float32)
    tri3 = np.concatenate([tri] * 3, axis=1)
    expand = np.zeros((LANES, D_INNER), np.float32)
    for part in range(3):
        for h in range(SSD_HEADS):
            expand[part * SSD_HEADS + h, h * SSD_HEAD_DIM:(h + 1) * SSD_HEAD_DIM] = 1.0
    col = np.arange(D_INNER)[None, :] % SSD_HEAD_DIM
    eye_tiled = (col == i).astype(np.float32)
    causal_tiled = (col <= i)
    causal_bias = np.where(causal_tiled, 0.0, -np.inf).astype(np.float32)
    return (jnp.asarray(tri3, BF16), jnp.asarray(expand, BF16),
            jnp.asarray(eye_tiled, F32), jnp.asarray(causal_bias, F32))


def _gla_kernel(x_ref, xold_ref, nw_ref, win_ref, wgk_ref, wgu_ref, bg_ref, hnw_ref, wout_ref,
                summat_ref, roleq_ref, rolek_ref, pmask_ref,
                out_ref,
                qk_a, v_a, g_a, a_a, o_a, qk_b, v_b, g_b, a_b, o_b, hn_s, gk_s, state_s, *, block_rows):
    set_a = (qk_a, v_a, g_a, a_a, o_a)
    set_b = (qk_b, v_b, g_b, a_b, o_b)

    @pl.when(pl.program_id(0) == 0)
    def _():
        state_s[...] = jnp.zeros_like(state_s)
        for buf in set_b:
            buf[...] = jnp.zeros_like(buf)

    def step(read_set, write_set):
        qk_s, v_s, g_s, a_s, o_old = read_set
        qk_n, v_n, g_n, a_n, o_s = write_set
        q_scale = GLA_HEAD_K ** -0.5
        n_chunks = block_rows // CHUNK
        decays = {}

        def chunk_decays(c):
            rows = slice(c * CHUNK, (c + 1) * CHUNK)
            a1, a2, a3 = _split3(a_s[rows, :])
            expo = _dot(summat_ref[...], jnp.concatenate([a1, a2, a3], axis=0))
            decays[c] = jnp.exp(expo)

        def head_scores(c, h):
            if h == 0 and c + 1 < n_chunks:
                chunk_decays(c + 1)
            rows = slice(c * CHUNK, (c + 1) * CHUNK)
            kl = slice(h * GLA_HEAD_K, (h + 1) * GLA_HEAD_K)
            decay = decays[c]
            qh = qk_s[rows, kl] * q_scale
            kh = qk_s[rows, GLA_DK + h * GLA_HEAD_K:GLA_DK + (h + 1) * GLA_HEAD_K]
            att = pmask_ref[0] * _dot_nt(qh.astype(BF16), kh.astype(BF16))
            for lv in range(len(GLA_LEVELS)):
                d_lv = decay[(2 + lv) * CHUNK:(3 + lv) * CHUNK, kl]
                fq = (qh * (d_lv * roleq_ref[lv])).astype(BF16)
                fk = (kh * (d_lv * rolek_ref[lv])).astype(BF16)
                att = att + pmask_ref[lv + 1] * _dot_nt(fq, fk)
            q_in = (qh * decay[0:CHUNK, kl]).astype(BF16)
            k_out = (kh * decay[CHUNK:2 * CHUNK, kl]).astype(BF16)
            return att.astype(BF16), q_in, k_out

        def head_output(c, h, scores):
            att, q_in, k_out = scores
            rows = slice(c * CHUNK, (c + 1) * CHUNK)
            kl = slice(h * GLA_HEAD_K, (h + 1) * GLA_HEAD_K)
            vl = slice(h * GLA_HEAD_V, (h + 1) * GLA_HEAD_V)
            state = state_s[h]
            vh = v_s[rows, vl]
            o = _dot(q_in, state.astype(BF16)) + _dot(att, vh)
            d_last = jnp.broadcast_to(decays[c][CHUNK - 1:CHUNK, kl], (GLA_HEAD_K, GLA_HEAD_K)).T
            d_last = jnp.concatenate([d_last] * (GLA_HEAD_V // GLA_HEAD_K), axis=1)
            state_s[h] = state * d_last + _dot_tn(k_out, vh)
            on = o * _rms_scale(o) * hnw_ref[...]
            g = g_s[rows, vl]
            o_s[rows, vl] = (on * (g * _sigmoid(g))).astype(BF16)

        x = x_ref[...]
        hn_s[...] = (x * _rms_scale(x) * nw_ref[...]).astype(BF16)

        def proj_piece(dst, first_col, j):
            cols = slice(j * PIECE_COLS, (j + 1) * PIECE_COLS)
            w_cols = slice(first_col + j * PIECE_COLS, first_col + (j + 1) * PIECE_COLS)
            def run():
                dst[:, cols] = _dot(hn_s[...], win_ref[:, w_cols]).astype(dst.dtype)
            return run

        def gate_low_piece():
            gk_s[...] = _dot(hn_s[...], wgk_ref[...]).astype(BF16)

        def gate_piece():
            logit = _dot(gk_s[...], wgu_ref[...]) + bg_ref[...]
            log_sig = jnp.minimum(logit, 0.0) - jnp.log1p(jnp.exp(-jnp.abs(logit)))
            a_n[...] = log_sig * (1.0 / GLA_GATE_NORMALIZER)

        def out_piece(j):
            cols = slice(j * PIECE_COLS, (j + 1) * PIECE_COLS)
            def run():
                out_ref[:, cols] = xold_ref[:, cols] + _dot(o_old[...], wout_ref[:, cols])
            return run

        pieces = ([(2, out_piece(j)) for j in range(D_MODEL // PIECE_COLS)]
                  + [(0.25, gate_low_piece)]
                  + [(1, proj_piece(qk_n, 0, j)) for j in range(2 * GLA_DK // PIECE_COLS)]
                  + [(0.5, gate_piece)]
                  + [(1, proj_piece(v_n, 2 * GLA_DK, j)) for j in range(GLA_DV // PIECE_COLS)]
                  + [(1, proj_piece(g_n, 2 * GLA_DK + GLA_DV, j)) for j in range(GLA_DV // PIECE_COLS)])
        tasks = [(c, h) for c in range(n_chunks) for h in range(GLA_HEADS)]
        chunk_decays(0)
        _run_pipeline(tasks, _piece_scheduler(pieces, 2 * (len(tasks) + PIPE_LAG)), head_scores, head_output)

    _by_parity(step, set_a, set_b)


def _const_spec(arr):
    nd = arr.ndim
    return pl.BlockSpec(arr.shape, lambda i, _nd=nd: (0,) * _nd, pipeline_mode=pl.Buffered(1))


def _pipelined_call(kernel_fn, name, x, operands, buffer_set, extra_scratch, block_rows):
    seq = x.shape[0]
    t = block_rows
    nblk = seq // t
    cur_spec = pl.BlockSpec((t, D_MODEL), lambda i: (jnp.minimum(i, nblk - 1), 0))
    old_spec = pl.BlockSpec((t, D_MODEL), lambda i: (jnp.maximum(i - 2, 0), 0))
    return pl.pallas_call(
        functools.partial(kernel_fn, block_rows=t),
        grid=(nblk + 2,),
        in_specs=[cur_spec, old_spec] + [_const_spec(a) for a in operands],
        out_specs=old_spec,
        out_shape=jax.ShapeDtypeStruct((seq, D_MODEL), F32),
        scratch_shapes=buffer_set + buffer_set + extra_scratch,
        compiler_params=pltpu.CompilerParams(dimension_semantics=("arbitrary",),
                                             vmem_limit_bytes=VMEM_LIMIT_BYTES),
        name=name,
    )(x, x, *operands)


def _gla_layer(x, norm_w, w_in, w_gate_up, b_gate_up, w_head_norm, w_out, *, block_rows):
    t = block_rows
    wgk = jnp.pad(w_in[:, 2 * GLA_DK + 2 * GLA_DV:], ((0, 0), (0, LANES - GLA_GATE_RANK))).astype(BF16)
    wgu = jnp.pad(w_gate_up, ((0, LANES - GLA_GATE_RANK), (0, 0))).astype(BF16)
    operands = (norm_w.reshape(1, D_MODEL), w_in.astype(BF16), wgk, wgu, b_gate_up.reshape(1, GLA_DK),
                w_head_norm.reshape(1, GLA_HEAD_V), w_out.astype(BF16)) + _gla_constants()
    buffer_set = [
        pltpu.VMEM((t, 2 * GLA_DK), F32),
        pltpu.VMEM((t, GLA_DV), BF16),
        pltpu.VMEM((t, GLA_DV), F32),
        pltpu.VMEM((t, GLA_DK), F32),
        pltpu.VMEM((t, GLA_DV), BF16),
    ]
    extra = [
        pltpu.VMEM((t, D_MODEL), BF16),
        pltpu.VMEM((t, LANES), BF16),
        pltpu.VMEM((GLA_HEADS, GLA_HEAD_K, GLA_HEAD_V), F32),
    ]
    return _pipelined_call(_gla_kernel, "gla_layer", x, operands, buffer_set, extra, t)


def _ssd_kernel(x_ref, xold_ref, nw_ref, win_ref, wdt_ref, convw_ref, convb_ref, dtb_ref, alog_ref,
                dskip_ref, gnw_ref, wout_ref, fnw_ref,
                tri3_ref, expand_ref, eye_ref, cbias_ref,
                out_ref,
                z_a, xs_a, bc_a, acol_a, dtx_a, y_a, z_b, xs_b, bc_b, acol_b, dtx_b, y_b,
                hn_s, halo_s, da_s, heads_s, state_s, *, block_rows):
    t = block_rows
    set_a = (z_a, xs_a, bc_a, acol_a, dtx_a, y_a)
    set_b = (z_b, xs_b, bc_b, acol_b, dtx_b, y_b)

    @pl.when(pl.program_id(0) == 0)
    def _():
        state_s[...] = jnp.zeros_like(state_s)
        halo_s[...] = jnp.zeros_like(halo_s)
        for buf in set_b:
            buf[...] = jnp.zeros_like(buf)

    def step(read_set, write_set):
        z_s, xs_s, bc_s, acol_s, dtx_s, y_old = read_set
        z_n, xs_n, bc_n, acol_n, dtx_n, y_s = write_set
        n_chunks = t // CHUNK

        x = x_ref[...]
        hn_s[...] = (x * _rms_scale(x) * nw_ref[...]).astype(BF16)

        def z_piece(j):
            cols = slice(j * PIECE_COLS, (j + 1) * PIECE_COLS)
            def run():
                z_n[:, cols] = _dot(hn_s[...], win_ref[:, cols])
            return run

        def conv_piece(j):
            cols = slice(j * PIECE_COLS, (j + 1) * PIECE_COLS)
            w_cols = slice(D_INNER + j * PIECE_COLS, D_INNER + (j + 1) * PIECE_COLS)
            def run():
                raw = _dot(hn_s[...], win_ref[:, w_cols])
                ext = jnp.concatenate([halo_s[:, cols], raw], axis=0)
                conv = convb_ref[:, cols] + convw_ref[SSD_CONV - 1:SSD_CONV, cols] * raw
                for w in range(SSD_CONV - 1):
                    shifted = pltpu.roll(ext, SSD_CONV - 1 - w, 0)[CONV_HALO:, :]
                    conv = conv + convw_ref[w:w + 1, cols] * shifted
                halo_s[:, cols] = raw[t - CONV_HALO:, :]
                conv = conv * _sigmoid(conv)
                if j * PIECE_COLS < D_INNER:
                    xs_n[:, cols] = conv
                else:
                    bc_n[:, j * PIECE_COLS - D_INNER:(j + 1) * PIECE_COLS - D_INNER] = conv.astype(BF16)
            return run

        lane = lax.broadcasted_iota(jnp.int32, (t, LANES), 1)

        def lane_groups(v):
            v1, v2, v3 = _split3(v)
            return jnp.where(lane < SSD_HEADS, v1, jnp.where(lane < 2 * SSD_HEADS, v2, v3))

        def dt_piece():
            dt_in = _dot(hn_s[...], wdt_ref[...]) + dtb_ref[...]
            dt = jnp.maximum(dt_in, 0.0) + jnp.log1p(jnp.exp(-jnp.abs(dt_in)))
            heads_s[1] = lane_groups(dt)
            da_s[...] = dt * (-jnp.exp(alog_ref[...]))

        def cumsum_piece():
            a_cum = []
            for c in range(n_chunks):
                a1, a2, a3 = _split3(da_s[c * CHUNK:(c + 1) * CHUNK, :])
                a_cum.append(_dot(tri3_ref[...], jnp.concatenate([a1, a2, a3], axis=0)))
            heads_s[0] = lane_groups(jnp.concatenate(a_cum, axis=0))

        def expand_piece(dst, src, j):
            cols = slice(j * EXPAND_COLS, (j + 1) * EXPAND_COLS)
            def run():
                dst[:, cols] = _dot(heads_s[src], expand_ref[:, cols])
            return run

        def out_piece(j):
            cols = slice(j * PIECE_COLS, (j + 1) * PIECE_COLS)
            def run():
                out_ref[:, cols] = xold_ref[:, cols] + _dot(y_old[...], wout_ref[:, cols])
            return run

        def final_norm_piece():
            res = out_ref[...]
            out_ref[...] = res * _rms_scale(res) * fnw_ref[...]

        n_exp = D_INNER // EXPAND_COLS
        pieces = _merge_evenly(
            [(2, out_piece(j)) for j in range(D_MODEL // PIECE_COLS)] + [(0.25, final_norm_piece)],
            [(1, conv_piece(j)) for j in range(SSD_CONV_DIM // PIECE_COLS)],
            [(1, z_piece(j)) for j in range(D_INNER // PIECE_COLS)],
            [(0.25, dt_piece), (0.25, cumsum_piece)]
            + [(0.5, expand_piece(acol_n, 0, j)) for j in range(n_exp)]
            + [(0.5, expand_piece(dtx_n, 1, j)) for j in range(n_exp)])

        prep = {}
        lane2 = lax.broadcasted_iota(jnp.int32, (CHUNK, LANES), 1)

        def chunk_prep(c):
            rows = slice(c * CHUNK, (c + 1) * CHUNK)
            acol = acol_s[rows, :]
            xs = xs_s[rows, :]
            xdt = xs * dtx_s[rows, :]
            arow = jnp.sum(acol * eye_ref[...], axis=0, keepdims=True)
            a_last = acol[CHUNK - 1:CHUNK, :]
            prep[c] = dict(
                xs=xs,
                lmat=jnp.exp(acol - arow + cbias_ref[...]),
                from_start=jnp.exp(acol),
                xw=(xdt * jnp.exp(a_last - acol)).astype(BF16),
                xdt_b=xdt.astype(BF16),
                chunk_decay=jnp.exp(a_last),
                y=[])

        def group_scores(c, g):
            if g == SSD_GROUPS // 2 and c + 1 < n_chunks:
                chunk_prep(c + 1)
            rows = slice(c * CHUNK, (c + 1) * CHUNK)
            b_g = bc_s[rows, g * SSD_STATE:(g + 1) * SSD_STATE]
            c_g = bc_s[rows, (SSD_GROUPS + g) * SSD_STATE:(SSD_GROUPS + g + 1) * SSD_STATE]
            cb2 = _dot_nt(c_g, jnp.concatenate([b_g, b_g], axis=0))
            return b_g, c_g, cb2

        def group_output(c, g, scores):
            b_g, c_g, cb2 = scores
            p = prep[c]
            gl = slice(g * SSD_GROUP_WIDTH, (g + 1) * SSD_GROUP_WIDTH)
            state = state_s[g]
            y_g = _dot(c_g, state.astype(BF16)) * p["from_start"][:, gl]
            diag = []
            for half in range(SSD_GROUP_WIDTH // LANES):
                tl = slice(g * SSD_GROUP_WIDTH + half * LANES, g * SSD_GROUP_WIDTH + (half + 1) * LANES)
                m_pair = (cb2 * p["lmat"][:, tl]).astype(BF16)
                x_pair = p["xdt_b"][:, tl]
                zero = jnp.zeros_like(x_pair)
                rhs = jnp.concatenate([jnp.where(lane2 < SSD_HEAD_DIM, x_pair, zero),
                                       jnp.where(lane2 < SSD_HEAD_DIM, zero, x_pair)], axis=0)
                diag.append(_dot(m_pair, rhs))
            p["y"].append(y_g + jnp.concatenate(diag, axis=1))
            state_s[g] = state * p["chunk_decay"][:, gl] + _dot_tn(b_g, p["xw"][:, gl])
            if g == SSD_GROUPS - 1:
                rows = slice(c * CHUNK, (c + 1) * CHUNK)
                y = jnp.concatenate(p["y"], axis=1) + p["xs"] * dskip_ref[...]
                z = z_s[rows, :]
                y = y * (z * _sigmoid(z))
                y_s[rows, :] = (y * _rms_scale(y) * gnw_ref[...]).astype(BF16)
                del prep[c]

        tasks = [(c, g) for c in range(n_chunks) for g in range(SSD_GROUPS)]
        chunk_prep(0)
        _run_pipeline(tasks, _piece_scheduler(pieces, 2 * (len(tasks) + PIPE_LAG)), group_scores, group_output)

    _by_parity(step, set_a, set_b)


def _ssd_layer(x, norm_w, w_in, conv_w, conv_b, dt_bias, a_log, d_skip, w_gate_norm, w_out, final_norm,
               *, block_rows):
    t = block_rows
    w_dt = w_in[:, D_INNER + SSD_CONV_DIM:]
    pad = jnp.zeros((D_MODEL, LANES - 3 * SSD_HEADS), w_dt.dtype)
    wdt = jnp.concatenate([w_dt, w_dt, w_dt, pad], axis=1).astype(BF16)

    def head_lanes(v):
        return jnp.concatenate([v, v, v, jnp.zeros((LANES - 3 * SSD_HEADS,), v.dtype)]).reshape(1, LANES)

    operands = (norm_w.reshape(1, D_MODEL), w_in.astype(BF16), wdt, conv_w, conv_b.reshape(1, SSD_CONV_DIM),
                head_lanes(dt_bias), head_lanes(a_log),
                jnp.repeat(d_skip, SSD_HEAD_DIM).reshape(1, D_INNER),
                w_gate_norm.reshape(1, D_INNER), w_out.astype(BF16),
                final_norm.reshape(1, D_MODEL)) + _ssd_constants()
    buffer_set = [
        pltpu.VMEM((t, D_INNER), F32),
        pltpu.VMEM((t, D_INNER), F32),
        pltpu.VMEM((t, 2 * SSD_GROUPS * SSD_STATE), BF16),
        pltpu.VMEM((t, D_INNER), F32),
        pltpu.VMEM((t, D_INNER), F32),
        pltpu.VMEM((t, D_INNER), BF16),
    ]
    extra = [
        pltpu.VMEM((t, D_MODEL), BF16),
        pltpu.VMEM((CONV_HALO, SSD_CONV_DIM), F32),
        pltpu.VMEM((t, LANES), F32),
        pltpu.VMEM((2, t, LANES), BF16),
        pltpu.VMEM((SSD_GROUPS, SSD_STATE, SSD_GROUP_WIDTH), F32),
    ]
    return _pipelined_call(_ssd_kernel, "ssd_layer", x, operands, buffer_set, extra, t)


def kernel(x, norm_w, gla_in_proj, gla_gate_up, gla_gate_bias, gla_head_norm, gla_out_proj, ssd_in_proj,
           ssd_conv_w, ssd_conv_b, ssd_dt_bias, ssd_a_log, ssd_d, ssd_gate_norm, ssd_out_proj, final_norm):
    bsz, seq, _ = x.shape
    assert bsz == 1 and seq % BLOCK_ROWS == 0
    h = x.reshape(seq, D_MODEL)
    h = _gla_layer(h, norm_w[0], gla_in_proj[0], gla_gate_up[0], gla_gate_bias[0], gla_head_norm[0],
                   gla_out_proj[0], block_rows=BLOCK_ROWS)
    h = _ssd_layer(h, norm_w[1], ssd_in_proj[0], ssd_conv_w[0], ssd_conv_b[0], ssd_dt_bias[0], ssd_a_log[0],
                   ssd_d[0], ssd_gate_norm[0], ssd_out_proj[0], final_norm, block_rows=BLOCK_ROWS)
    return h.reshape(bsz, seq, D_MODEL)
```

```python
import functools

import numpy as np
import jax
import jax.numpy as jnp
from jax import lax
from jax.experimental import pallas as pl
from jax.experimental.pallas import tpu as pltpu

F32 = jnp.float32
BF16 = jnp.bfloat16

D_MODEL = 1024
D_INNER = 2048
RMS_EPS = 1e-6
CHUNK = 64

GLA_HEADS = 4
GLA_DK = 512
GLA_DV = 2048
GLA_HEAD_K = 128
GLA_HEAD_V = 512
GLA_GATE_RANK = 16
GLA_GATE_NORMALIZER = 16.0
GLA_LEVELS = (32, 16, 8, 4, 2, 1)

SSD_HEAD_DIM = 64
SSD_HEADS = 32
SSD_GROUPS = 8
SSD_STATE = 128
SSD_CONV = 4
SSD_CONV_DIM = 4096
SSD_GROUP_WIDTH = (SSD_HEADS // SSD_GROUPS) * SSD_HEAD_DIM

LANES = 128
CONV_HALO = 8
BLOCK_ROWS = 256
PIECE_COLS = 256
EXPAND_COLS = 512
PIPE_LAG = 1
VMEM_LIMIT_BYTES = 60 * 1024 * 1024


def _dot(a, b):
    return jnp.dot(a, b, preferred_element_type=F32)


def _dot_nt(a, b):
    return lax.dot_general(a, b, (((1,), (1,)), ((), ())), preferred_element_type=F32)


def _dot_tn(a, b):
    return lax.dot_general(a, b, (((0,), (0,)), ((), ())), preferred_element_type=F32)


def _split3(x):
    x1 = x.astype(BF16)
    r1 = x - x1.astype(F32)
    x2 = r1.astype(BF16)
    x3 = (r1 - x2.astype(F32)).astype(BF16)
    return x1, x2, x3


def _rms_scale(x):
    return lax.rsqrt(jnp.mean(x * x, axis=-1, keepdims=True) + RMS_EPS)


def _sigmoid(x):
    return 1.0 / (1.0 + jnp.exp(-x))


def _piece_scheduler(pieces, n_calls):
    total = sum(w for w, _ in pieces)
    state = {"call": 0, "next": 0, "done": 0}

    def run():
        state["call"] += 1
        target = total * state["call"] / n_calls
        while state["next"] < len(pieces) and (state["done"] < target or state["call"] == n_calls):
            w, fn = pieces[state["next"]]
            fn()
            state["done"] += w
            state["next"] += 1
    return run


def _merge_evenly(*streams):
    keyed = [((k + (0.5 if si else 0.0)) / len(s), si, k) for si, s in enumerate(streams) for k in range(len(s))]
    return [streams[si][k] for _, si, k in sorted(keyed)]


def _run_pipeline(tasks, run_pieces, scores_fn, output_fn):
    n_slots = len(tasks) + PIPE_LAG
    scores = {}
    for slot in range(n_slots):
        run_pieces()
        if slot < len(tasks):
            scores[slot] = scores_fn(*tasks[slot])
        run_pieces()
        if slot >= PIPE_LAG:
            output_fn(*tasks[slot - PIPE_LAG], scores.pop(slot - PIPE_LAG))


ALL_STAGES = ("project", "recur", "output")


def _dispatch_steps(step_fn, set_a, set_b, n_steps):
    step_id = pl.program_id(0)
    last = n_steps - 1
    sets = lambda parity: (set_b, set_a) if parity == 0 else (set_a, set_b)

    @pl.when(step_id == 0)
    def _():
        step_fn(*sets(0), ("project",))

    @pl.when(step_id == last)
    def _():
        step_fn(*sets(last % 2), ("output",))

    for parity in (0, 1):
        @pl.when((step_id % 2 == parity) & (step_id > 0) & (step_id < last))
        def _():
            step_fn(*sets(parity), ALL_STAGES)


def _gla_constants():
    c = CHUNK
    i = np.arange(c)[:, None]
    m = np.arange(c)[None, :]
    sets = [m <= i, m > i]
    roles_q, roles_k, pair_masks = [], [], [(i == m)]
    for s in GLA_LEVELS:
        blk = i // s
        lower = (blk % 2) == 1
        prefix = (m >= blk * s) & (m <= i)
        suffix = (m > i) & (m < (blk + 1) * s)
        sets.append(np.where(lower, prefix, suffix))
        roles_q.append(np.broadcast_to(lower, (c, LANES)))
        roles_k.append(np.broadcast_to(~lower, (c, LANES)))
        pair_masks.append(lower & ((m // s) == blk - 1))
    sum_mat = np.concatenate(sets, axis=0).astype(np.float32)
    sum_mat = np.concatenate([sum_mat] * 3, axis=1)
    return (jnp.asarray(sum_mat, BF16),
            jnp.asarray(np.stack(roles_q), F32), jnp.asarray(np.stack(roles_k), F32),
            jnp.asarray(np.stack(pair_masks), F32))


def _ssd_constants():
    c = CHUNK
    i = np.arange(c)[:, None]
    m = np.arange(c)[None, :]
    tri = (m <= i).astype(np.float32)
    tri3 = np.concatenate([tri] * 3, axis=1)
    expand = np.zeros((LANES, D_INNER), np.float32)
    for part in range(3):
        for h in range(SSD_HEADS):
            expand[part * SSD_HEADS + h, h * SSD_HEAD_DIM:(h + 1) * SSD_HEAD_DIM] = 1.0
    col = np.arange(D_INNER)[None, :] % SSD_HEAD_DIM
    eye_tiled = (col == i).astype(np.float32)
    causal_tiled = (col <= i)
    causal_bias = np.where(causal_tiled, 0.0, -np.inf).astype(np.float32)
    return (jnp.asarray(tri3, BF16), jnp.asarray(expand, BF16),
            jnp.asarray(eye_tiled, F32), jnp.asarray(causal_bias, F32))


def _gla_kernel(x_ref, xold_ref, nw_ref, win_ref, wgk_ref, wgu_ref, bg_ref, hnw_ref, wout_ref,
                summat_ref, roleq_ref, rolek_ref, pmask_ref,
                out_ref,
                qk_a, v_a, g_a, a_a, o_a, qk_b, v_b, g_b, a_b, o_b, hn_s, gk_s, state_s, *, block_rows, n_steps):
    set_a = (qk_a, v_a, g_a, a_a, o_a)
    set_b = (qk_b, v_b, g_b, a_b, o_b)

    @pl.when(pl.program_id(0) == 0)
    def _():
        state_s[...] = jnp.zeros_like(state_s)
        o_a[...] = jnp.zeros_like(o_a)

    def step(read_set, write_set, stages):
        qk_s, v_s, g_s, a_s, o_old = read_set
        qk_n, v_n, g_n, a_n, o_s = write_set
        q_scale = GLA_HEAD_K ** -0.5
        n_chunks = block_rows // CHUNK
        decays = {}

        def chunk_decays(c):
            rows = slice(c * CHUNK, (c + 1) * CHUNK)
            a1, a2, a3 = _split3(a_s[rows, :])
            expo = _dot(summat_ref[...], jnp.concatenate([a1, a2, a3], axis=0))
            decays[c] = jnp.exp(expo)

        def head_scores(c, h):
            if h == 0 and c + 1 < n_chunks:
                chunk_decays(c + 1)
            rows = slice(c * CHUNK, (c + 1) * CHUNK)
            kl = slice(h * GLA_HEAD_K, (h + 1) * GLA_HEAD_K)
            decay = decays[c]
            qh = qk_s[rows, kl] * q_scale
            kh = qk_s[rows, GLA_DK + h * GLA_HEAD_K:GLA_DK + (h + 1) * GLA_HEAD_K]
            att = pmask_ref[0] * _dot_nt(qh.astype(BF16), kh.astype(BF16))
            for lv in range(len(GLA_LEVELS)):
                d_lv = decay[(2 + lv) * CHUNK:(3 + lv) * CHUNK, kl]
                fq = (qh * (d_lv * roleq_ref[lv])).astype(BF16)
                fk = (kh * (d_lv * rolek_ref[lv])).astype(BF16)
                att = att + pmask_ref[lv + 1] * _dot_nt(fq, fk)
            q_in = (qh * decay[0:CHUNK, kl]).astype(BF16)
            k_out = (kh * decay[CHUNK:2 * CHUNK, kl]).astype(BF16)
            return att.astype(BF16), q_in, k_out

        def head_output(c, h, scores):
            att, q_in, k_out = scores
            rows = slice(c * CHUNK, (c + 1) * CHUNK)
            kl = slice(h * GLA_HEAD_K, (h + 1) * GLA_HEAD_K)
            vl = slice(h * GLA_HEAD_V, (h + 1) * GLA_HEAD_V)
            state = state_s[h]
            vh = v_s[rows, vl]
            o = _dot(q_in, state.astype(BF16)) + _dot(att, vh)
            d_last = jnp.broadcast_to(decays[c][CHUNK - 1:CHUNK, kl], (GLA_HEAD_K, GLA_HEAD_K)).T
            d_last = jnp.concatenate([d_last] * (GLA_HEAD_V // GLA_HEAD_K), axis=1)
            state_s[h] = state * d_last + _dot_tn(k_out, vh)
            on = o * _rms_scale(o) * hnw_ref[...]
            g = g_s[rows, vl]
            o_s[rows, vl] = (on * (g * _sigmoid(g))).astype(BF16)

        if "project" in stages:
            x = x_ref[...]
            hn_s[...] = (x * _rms_scale(x) * nw_ref[...]).astype(BF16)

        def proj_piece(dst, first_col, j):
            cols = slice(j * PIECE_COLS, (j + 1) * PIECE_COLS)
            w_cols = slice(first_col + j * PIECE_COLS, first_col + (j + 1) * PIECE_COLS)
            def run():
                dst[:, cols] = _dot(hn_s[...], win_ref[:, w_cols]).astype(dst.dtype)
            return run

        def gate_low_piece():
            gk_s[...] = _dot(hn_s[...], wgk_ref[...]).astype(BF16)

        def gate_piece():
            logit = _dot(gk_s[...], wgu_ref[...]) + bg_ref[...]
            log_sig = jnp.minimum(logit, 0.0) - jnp.log1p(jnp.exp(-jnp.abs(logit)))
            a_n[...] = log_sig * (1.0 / GLA_GATE_NORMALIZER)

        def out_piece(j):
            cols = slice(j * PIECE_COLS, (j + 1) * PIECE_COLS)
            def run():
                out_ref[:, cols] = xold_ref[:, cols] + _dot(o_old[...], wout_ref[:, cols])
            return run

        pieces = []
        if "output" in stages:
            pieces += [(2, out_piece(j)) for j in range(D_MODEL // PIECE_COLS)]
        if "project" in stages:
            pieces += ([(0.25, gate_low_piece)]
                       + [(1, proj_piece(qk_n, 0, j)) for j in range(2 * GLA_DK // PIECE_COLS)]
                       + [(0.5, gate_piece)]
                       + [(1, proj_piece(v_n, 2 * GLA_DK, j)) for j in range(GLA_DV // PIECE_COLS)]
                       + [(1, proj_piece(g_n, 2 * GLA_DK + GLA_DV, j)) for j in range(GLA_DV // PIECE_COLS)])
        tasks = []
        if "recur" in stages:
            tasks = [(c, h) for c in range(n_chunks) for h in range(GLA_HEADS)]
            chunk_decays(0)
        _run_pipeline(tasks, _piece_scheduler(pieces, 2 * (len(tasks) + PIPE_LAG)), head_scores, head_output)

    _dispatch_steps(step, set_a, set_b, n_steps)


def _const_spec(arr):
    nd = arr.ndim
    return pl.BlockSpec(arr.shape, lambda i, _nd=nd: (0,) * _nd, pipeline_mode=pl.Buffered(1))


def _pipelined_call(kernel_fn, name, x, operands, buffer_set, extra_scratch, block_rows):
    seq = x.shape[0]
    t = block_rows
    nblk = seq // t
    cur_spec = pl.BlockSpec((t, D_MODEL), lambda i: (jnp.minimum(i, nblk - 1), 0))
    old_spec = pl.BlockSpec((t, D_MODEL), lambda i: (jnp.maximum(i - 2, 0), 0))
    return pl.pallas_call(
        functools.partial(kernel_fn, block_rows=t, n_steps=nblk + 2),
        grid=(nblk + 2,),
        in_specs=[cur_spec, old_spec] + [_const_spec(a) for a in operands],
        out_specs=old_spec,
        out_shape=jax.ShapeDtypeStruct((seq, D_MODEL), F32),
        scratch_shapes=buffer_set + buffer_set + extra_scratch,
        compiler_params=pltpu.CompilerParams(dimension_semantics=("arbitrary",),
                                             vmem_limit_bytes=VMEM_LIMIT_BYTES),
        name=name,
    )(x, x, *operands)


def _gla_layer(x, norm_w, w_in, w_gate_up, b_gate_up, w_head_norm, w_out, *, block_rows):
    t = block_rows
    wgk = jnp.pad(w_in[:, 2 * GLA_DK + 2 * GLA_DV:], ((0, 0), (0, LANES - GLA_GATE_RANK))).astype(BF16)
    wgu = jnp.pad(w_gate_up, ((0, LANES - GLA_GATE_RANK), (0, 0))).astype(BF16)
    operands = (norm_w.reshape(1, D_MODEL), w_in.astype(BF16), wgk, wgu, b_gate_up.reshape(1, GLA_DK),
                w_head_norm.reshape(1, GLA_HEAD_V), w_out.astype(BF16)) + _gla_constants()
    buffer_set = [
        pltpu.VMEM((t, 2 * GLA_DK), F32),
        pltpu.VMEM((t, GLA_DV), BF16),
        pltpu.VMEM((t, GLA_DV), F32),
        pltpu.VMEM((t, GLA_DK), F32),
        pltpu.VMEM((t, GLA_DV), BF16),
    ]
    extra = [
        pltpu.VMEM((t, D_MODEL), BF16),
        pltpu.VMEM((t, LANES), BF16),
        pltpu.VMEM((GLA_HEADS, GLA_HEAD_K, GLA_HEAD_V), F32),
    ]
    return _pipelined_call(_gla_kernel, "gla_layer", x, operands, buffer_set, extra, t)


def _ssd_kernel(x_ref, xold_ref, nw_ref, win_ref, wdt_ref, convw_ref, convb_ref, dtb_ref, alog_ref,
                dskip_ref, gnw_ref, wout_ref, fnw_ref,
                tri3_ref, expand_ref, eye_ref, cbias_ref,
                out_ref,
                z_a, xs_a, bc_a, acol_a, dtx_a, y_a, z_b, xs_b, bc_b, acol_b, dtx_b, y_b,
                hn_s, halo_s, da_s, heads_s, state_s, *, block_rows, n_steps):
    t = block_rows
    set_a = (z_a, xs_a, bc_a, acol_a, dtx_a, y_a)
    set_b = (z_b, xs_b, bc_b, acol_b, dtx_b, y_b)

    @pl.when(pl.program_id(0) == 0)
    def _():
        state_s[...] = jnp.zeros_like(state_s)
        halo_s[...] = jnp.zeros_like(halo_s)
        y_a[...] = jnp.zeros_like(y_a)

    def step(read_set, write_set, stages):
        z_s, xs_s, bc_s, acol_s, dtx_s, y_old = read_set
        z_n, xs_n, bc_n, acol_n, dtx_n, y_s = write_set
        n_chunks = t // CHUNK

        if "project" in stages:
            x = x_ref[...]
            hn_s[...] = (x * _rms_scale(x) * nw_ref[...]).astype(BF16)

        def z_piece(j):
            cols = slice(j * PIECE_COLS, (j + 1) * PIECE_COLS)
            def run():
                z_n[:, cols] = _dot(hn_s[...], win_ref[:, cols])
            return run

        def conv_piece(j):
            cols = slice(j * PIECE_COLS, (j + 1) * PIECE_COLS)
            w_cols = slice(D_INNER + j * PIECE_COLS, D_INNER + (j + 1) * PIECE_COLS)
            def run():
                raw = _dot(hn_s[...], win_ref[:, w_cols])
                ext = jnp.concatenate([halo_s[:, cols], raw], axis=0)
                conv = convb_ref[:, cols] + convw_ref[SSD_CONV - 1:SSD_CONV, cols] * raw
                for w in range(SSD_CONV - 1):
                    shifted = pltpu.roll(ext, SSD_CONV - 1 - w, 0)[CONV_HALO:, :]
                    conv = conv + convw_ref[w:w + 1, cols] * shifted
                halo_s[:, cols] = raw[t - CONV_HALO:, :]
                conv = conv * _sigmoid(conv)
                if j * PIECE_COLS < D_INNER:
                    xs_n[:, cols] = conv
                else:
                    bc_n[:, j * PIECE_COLS - D_INNER:(j + 1) * PIECE_COLS - D_INNER] = conv.astype(BF16)
            return run

        lane = lax.broadcasted_iota(jnp.int32, (t, LANES), 1)

        def lane_groups(v):
            v1, v2, v3 = _split3(v)
            return jnp.where(lane < SSD_HEADS, v1, jnp.where(lane < 2 * SSD_HEADS, v2, v3))

        def dt_piece():
            dt_in = _dot(hn_s[...], wdt_ref[...]) + dtb_ref[...]
            dt = jnp.maximum(dt_in, 0.0) + jnp.log1p(jnp.exp(-jnp.abs(dt_in)))
            heads_s[1] = lane_groups(dt)
            da_s[...] = dt * (-jnp.exp(alog_ref[...]))

        def cumsum_piece():
            a_cum = []
            for c in range(n_chunks):
                a1, a2, a3 = _split3(da_s[c * CHUNK:(c + 1) * CHUNK, :])
                a_cum.append(_dot(tri3_ref[...], jnp.concatenate([a1, a2, a3], axis=0)))
            heads_s[0] = lane_groups(jnp.concatenate(a_cum, axis=0))

        def expand_piece(dst, src, j):
            cols = slice(j * EXPAND_COLS, (j + 1) * EXPAND_COLS)
            def run():
                dst[:, cols] = _dot(heads_s[src], expand_ref[:, cols])
            return run

        def out_piece(j):
            cols = slice(j * PIECE_COLS, (j + 1) * PIECE_COLS)
            def run():
                out_ref[:, cols] = xold_ref[:, cols] + _dot(y_old[...], wout_ref[:, cols])
            return run

        def final_norm_piece():
            res = out_ref[...]
            out_ref[...] = res * _rms_scale(res) * fnw_ref[...]

        n_exp = D_INNER // EXPAND_COLS
        streams = []
        if "output" in stages:
            streams += [[(2, out_piece(j)) for j in range(D_MODEL // PIECE_COLS)] + [(0.25, final_norm_piece)]]
        if "project" in stages:
            streams += [[(1, conv_piece(j)) for j in range(SSD_CONV_DIM // PIECE_COLS)],
                        [(1, z_piece(j)) for j in range(D_INNER // PIECE_COLS)],
                        [(0.25, dt_piece), (0.25, cumsum_piece)]
                        + [(0.5, expand_piece(acol_n, 0, j)) for j in range(n_exp)]
                        + [(0.5, expand_piece(dtx_n, 1, j)) for j in range(n_exp)]]
        pieces = _merge_evenly(*streams)

        prep = {}
        lane2 = lax.broadcasted_iota(jnp.int32, (CHUNK, LANES), 1)

        def group_prep(c, g):
            rows = slice(c * CHUNK, (c + 1) * CHUNK)
            gl = slice(g * SSD_GROUP_WIDTH, (g + 1) * SSD_GROUP_WIDTH)
            acol = acol_s[rows, gl]
            xs = xs_s[rows, gl]
            xdt = xs * dtx_s[rows, gl]
            arow = jnp.sum(acol * eye_ref[:, gl], axis=0, keepdims=True)
            a_last = acol[CHUNK - 1:CHUNK, :]
            prep[(c, g)] = dict(
                xs=xs,
                lmat=jnp.exp(acol - arow + cbias_ref[:, gl]),
                from_start=jnp.exp(acol),
                xw=(xdt * jnp.exp(a_last - acol)).astype(BF16),
                xdt_b=xdt.astype(BF16),
                chunk_decay=jnp.exp(a_last))

        def group_scores(c, g):
            if c + 1 < n_chunks:
                group_prep(c + 1, g)
            rows = slice(c * CHUNK, (c + 1) * CHUNK)
            b_g = bc_s[rows, g * SSD_STATE:(g + 1) * SSD_STATE]
            c_g = bc_s[rows, (SSD_GROUPS + g) * SSD_STATE:(SSD_GROUPS + g + 1) * SSD_STATE]
            cb2 = _dot_nt(c_g, jnp.concatenate([b_g, b_g], axis=0))
            return b_g, c_g, cb2

        def group_output(c, g, scores):
            b_g, c_g, cb2 = scores
            p = prep.pop((c, g))
            rows = slice(c * CHUNK, (c + 1) * CHUNK)
            gl = slice(g * SSD_GROUP_WIDTH, (g + 1) * SSD_GROUP_WIDTH)
            state = state_s[g]
            y_g = _dot(c_g, state.astype(BF16)) * p["from_start"]
            diag = []
            for half in range(SSD_GROUP_WIDTH // LANES):
                tl = slice(half * LANES, (half + 1) * LANES)
                m_pair = (cb2 * p["lmat"][:, tl]).astype(BF16)
                x_pair = p["xdt_b"][:, tl]
                zero = jnp.zeros_like(x_pair)
                rhs = jnp.concatenate([jnp.where(lane2 < SSD_HEAD_DIM, x_pair, zero),
                                       jnp.where(lane2 < SSD_HEAD_DIM, zero, x_pair)], axis=0)
                diag.append(_dot(m_pair, rhs))
            state_s[g] = state * p["chunk_decay"] + _dot_tn(b_g, p["xw"])
            y_g = y_g + jnp.concatenate(diag, axis=1) + p["xs"] * dskip_ref[:, gl]
            z = z_s[rows, gl]
            y_g = y_g * (z * _sigmoid(z))
            gated[(c, g)] = y_g
            sumsq[(c, g)] = jnp.sum(y_g * y_g, axis=-1, keepdims=True)
            if g == SSD_GROUPS - 1:
                total = sumsq.pop((c, 0))
                for k in range(1, SSD_GROUPS):
                    total = total + sumsq.pop((c, k))
                scale = lax.rsqrt(total * (1.0 / D_INNER) + RMS_EPS)
                for k in range(SSD_GROUPS):
                    kl = slice(k * SSD_GROUP_WIDTH, (k + 1) * SSD_GROUP_WIDTH)
                    y_s[rows, kl] = (gated.pop((c, k)) * scale * gnw_ref[:, kl]).astype(BF16)

        gated, sumsq = {}, {}
        tasks = []
        if "recur" in stages:
            tasks = [(c, g) for c in range(n_chunks) for g in range(SSD_GROUPS)]
            for g in range(SSD_GROUPS):
                group_prep(0, g)
        _run_pipeline(tasks, _piece_scheduler(pieces, 2 * (len(tasks) + PIPE_LAG)), group_scores, group_output)

    _dispatch_steps(step, set_a, set_b, n_steps)


def _ssd_layer(x, norm_w, w_in, conv_w, conv_b, dt_bias, a_log, d_skip, w_gate_norm, w_out, final_norm,
               *, block_rows):
    t = block_rows
    w_dt = w_in[:, D_INNER + SSD_CONV_DIM:]
    pad = jnp.zeros((D_MODEL, LANES - 3 * SSD_HEADS), w_dt.dtype)
    wdt = jnp.concatenate([w_dt, w_dt, w_dt, pad], axis=1).astype(BF16)

    def head_lanes(v):
        return jnp.concatenate([v, v, v, jnp.zeros((LANES - 3 * SSD_HEADS,), v.dtype)]).reshape(1, LANES)

    operands = (norm_w.reshape(1, D_MODEL), w_in.astype(BF16), wdt, conv_w, conv_b.reshape(1, SSD_CONV_DIM),
                head_lanes(dt_bias), head_lanes(a_log),
                jnp.repeat(d_skip, SSD_HEAD_DIM).reshape(1, D_INNER),
                w_gate_norm.reshape(1, D_INNER), w_out.astype(BF16),
                final_norm.reshape(1, D_MODEL)) + _ssd_constants()
    buffer_set = [
        pltpu.VMEM((t, D_INNER), F32),
        pltpu.VMEM((t, D_INNER), F32),
        pltpu.VMEM((t, 2 * SSD_GROUPS * SSD_STATE), BF16),
        pltpu.VMEM((t, D_INNER), F32),
        pltpu.VMEM((t, D_INNER), F32),
        pltpu.VMEM((t, D_INNER), BF16),
    ]
    extra = [
        pltpu.VMEM((t, D_MODEL), BF16),
        pltpu.VMEM((CONV_HALO, SSD_CONV_DIM), F32),
        pltpu.VMEM((t, LANES), F32),
        pltpu.VMEM((2, t, LANES), BF16),
        pltpu.VMEM((SSD_GROUPS, SSD_STATE, SSD_GROUP_WIDTH), F32),
    ]
    return _pipelined_call(_ssd_kernel, "ssd_layer", x, operands, buffer_set, extra, t)


def kernel(x, norm_w, gla_in_proj, gla_gate_up, gla_gate_bias, gla_head_norm, gla_out_proj, ssd_in_proj,
           ssd_conv_w, ssd_conv_b, ssd_dt_bias, ssd_a_log, ssd_d, ssd_gate_norm, ssd_out_proj, final_norm):
    bsz, seq, _ = x.shape
    assert bsz == 1 and seq % BLOCK_ROWS == 0
    h = x.reshape(seq, D_MODEL)
    h = _gla_layer(h, norm_w[0], gla_in_proj[0], gla_gate_up[0], gla_gate_bias[0], gla_head_norm[0],
                   gla_out_proj[0], block_rows=BLOCK_ROWS)
    h = _ssd_layer(h, norm_w[1], ssd_in_proj[0], ssd_conv_w[0], ssd_conv_b[0], ssd_dt_bias[0], ssd_a_log[0],
                   ssd_d[0], ssd_gate_norm[0], ssd_out_proj[0], final_norm, block_rows=BLOCK_ROWS)
    return h.reshape(bsz, seq, D_MODEL)
```

```python
import functools

import numpy as np
import jax
import jax.numpy as jnp
from jax import lax
from jax.experimental import pallas as pl
from jax.experimental.pallas import tpu as pltpu

F32 = jnp.float32
BF16 = jnp.bfloat16

D_MODEL = 1024
D_INNER = 2048
RMS_EPS = 1e-6
LOG2_E = 1.4426950408889634
CHUNK = 64

GLA_HEADS = 4
GLA_DK = 512
GLA_DV = 2048
GLA_HEAD_K = 128
GLA_HEAD_V = 512
GLA_GATE_RANK = 16
GLA_GATE_NORMALIZER = 16.0
GLA_LEVELS = (32, 16, 8, 4, 2, 1)

SSD_HEAD_DIM = 64
SSD_HEADS = 32
SSD_GROUPS = 8
SSD_STATE = 128
SSD_CONV = 4
SSD_CONV_DIM = 4096
SSD_GROUP_WIDTH = (SSD_HEADS // SSD_GROUPS) * SSD_HEAD_DIM

LANES = 128
CONV_HALO = 8
BLOCK_ROWS = 256
PIECE_COLS = 256
EXPAND_COLS = 512
PIPE_LAG = 1
VMEM_LIMIT_BYTES = 60 * 1024 * 1024


def _dot(a, b):
    return jnp.dot(a, b, preferred_element_type=F32)


def _dot_nt(a, b):
    return lax.dot_general(a, b, (((1,), (1,)), ((), ())), preferred_element_type=F32)


def _dot_tn(a, b):
    return lax.dot_general(a, b, (((0,), (0,)), ((), ())), preferred_element_type=F32)


def _split3(x):
    x1 = x.astype(BF16)
    r1 = x - x1.astype(F32)
    x2 = r1.astype(BF16)
    x3 = (r1 - x2.astype(F32)).astype(BF16)
    return x1, x2, x3


def _rms_scale(x):
    return lax.rsqrt(jnp.mean(x * x, axis=-1, keepdims=True) + RMS_EPS)


def _sigmoid(x):
    return 1.0 / (1.0 + jnp.exp2(x * -LOG2_E))


def _piece_scheduler(pieces, n_calls):
    total = sum(w for w, _ in pieces)
    state = {"call": 0, "next": 0, "done": 0}

    def run():
        state["call"] += 1
        target = total * state["call"] / n_calls
        while state["next"] < len(pieces) and (state["done"] < target or state["call"] == n_calls):
            w, fn = pieces[state["next"]]
            fn()
            state["done"] += w
            state["next"] += 1
    return run


def _merge_evenly(*streams):
    keyed = [((k + (0.5 if si else 0.0)) / len(s), si, k) for si, s in enumerate(streams) for k in range(len(s))]
    return [streams[si][k] for _, si, k in sorted(keyed)]


def _run_pipeline(tasks, run_pieces, scores_fn, output_fn):
    n_slots = len(tasks) + PIPE_LAG
    scores = {}
    for slot in range(n_slots):
        run_pieces()
        if slot < len(tasks):
            scores[slot] = scores_fn(*tasks[slot])
        run_pieces()
        if slot >= PIPE_LAG:
            output_fn(*tasks[slot - PIPE_LAG], scores.pop(slot - PIPE_LAG))


ALL_STAGES = ("project", "recur", "output")


def _dispatch_steps(step_fn, set_a, set_b, n_steps):
    step_id = pl.program_id(0)
    last = n_steps - 1
    sets = lambda parity: (set_b, set_a) if parity == 0 else (set_a, set_b)

    @pl.when(step_id == 0)
    def _():
        step_fn(*sets(0), ("project",))

    @pl.when(step_id == last)
    def _():
        step_fn(*sets(last % 2), ("output",))

    for parity in (0, 1):
        @pl.when((step_id % 2 == parity) & (step_id > 0) & (step_id < last))
        def _():
            step_fn(*sets(parity), ALL_STAGES)


def _gla_constants():
    c = CHUNK
    i = np.arange(c)[:, None]
    m = np.arange(c)[None, :]
    sets = [m <= i, m > i]
    roles_q, roles_k, pair_masks = [], [], [(i == m)]
    for s in GLA_LEVELS:
        blk = i // s
        lower = (blk % 2) == 1
        prefix = (m >= blk * s) & (m <= i)
        suffix = (m > i) & (m < (blk + 1) * s)
        sets.append(np.where(lower, prefix, suffix))
        roles_q.append(np.broadcast_to(lower, (c, LANES)))
        roles_k.append(np.broadcast_to(~lower, (c, LANES)))
        pair_masks.append(lower & ((m // s) == blk - 1))
    sum_mat = np.concatenate(sets, axis=0).astype(np.float32)
    sum_mat = np.concatenate([sum_mat] * 3, axis=1)
    return (jnp.asarray(sum_mat, BF16),
            jnp.asarray(np.stack(roles_q), F32), jnp.asarray(np.stack(roles_k), F32),
            jnp.asarray(np.stack(pair_masks), F32))


def _ssd_constants():
    c = CHUNK
    i = np.arange(c)[:, None]
    m = np.arange(c)[None, :]
    tri = (m <= i).astype(np.float32)
    tri3 = np.concatenate([tri] * 3, axis=1)
    expand = np.zeros((LANES, D_INNER), np.float32)
    for part in range(3):
        for h in range(SSD_HEADS):
            expand[part * SSD_HEADS + h, h * SSD_HEAD_DIM:(h + 1) * SSD_HEAD_DIM] = 1.0
    col = np.arange(D_INNER)[None, :] % SSD_HEAD_DIM
    eye_tiled = (col == i).astype(np.float32)
    causal_tiled = (col <= i)
    causal_bias = np.where(causal_tiled, 0.0, -np.inf).astype(np.float32)
    return (jnp.asarray(tri3, BF16), jnp.asarray(expand, BF16),
            jnp.asarray(eye_tiled, F32), jnp.asarray(causal_bias, F32))


def _gla_kernel(x_ref, xold_ref, nw_ref, win_ref, wgk_ref, wgu_ref, bg_ref, hnw_ref, wout_ref,
                summat_ref, roleq_ref, rolek_ref, pmask_ref,
                out_ref,
                qk_a, v_a, g_a, a_a, o_a, qk_b, v_b, g_b, a_b, o_b, hn_s, gk_s, state_s, *, block_rows, n_steps):
    set_a = (qk_a, v_a, g_a, a_a, o_a)
    set_b = (qk_b, v_b, g_b, a_b, o_b)

    @pl.when(pl.program_id(0) == 0)
    def _():
        state_s[...] = jnp.zeros_like(state_s)
        o_a[...] = jnp.zeros_like(o_a)

    def step(read_set, write_set, stages):
        qk_s, v_s, g_s, a_s, o_old = read_set
        qk_n, v_n, g_n, a_n, o_s = write_set
        q_scale = GLA_HEAD_K ** -0.5
        n_chunks = block_rows // CHUNK
        decays = {}

        def chunk_decays(c):
            rows = slice(c * CHUNK, (c + 1) * CHUNK)
            a1, a2, a3 = _split3(a_s[rows, :])
            expo = _dot(summat_ref[...], jnp.concatenate([a1, a2, a3], axis=0))
            decays[c] = jnp.exp2(expo)

        def head_scores(c, h):
            if h == 0 and c + 1 < n_chunks:
                chunk_decays(c + 1)
            rows = slice(c * CHUNK, (c + 1) * CHUNK)
            kl = slice(h * GLA_HEAD_K, (h + 1) * GLA_HEAD_K)
            decay = decays[c]
            qh = qk_s[rows, kl] * q_scale
            kh = qk_s[rows, GLA_DK + h * GLA_HEAD_K:GLA_DK + (h + 1) * GLA_HEAD_K]
            att = pmask_ref[0] * _dot_nt(qh.astype(BF16), kh.astype(BF16))
            for lv in range(len(GLA_LEVELS)):
                d_lv = decay[(2 + lv) * CHUNK:(3 + lv) * CHUNK, kl]
                fq = (qh * (d_lv * roleq_ref[lv])).astype(BF16)
                fk = (kh * (d_lv * rolek_ref[lv])).astype(BF16)
                att = att + pmask_ref[lv + 1] * _dot_nt(fq, fk)
            q_in = (qh * decay[0:CHUNK, kl]).astype(BF16)
            k_out = (kh * decay[CHUNK:2 * CHUNK, kl]).astype(BF16)
            return att.astype(BF16), q_in, k_out

        def head_output(c, h, scores):
            att, q_in, k_out = scores
            rows = slice(c * CHUNK, (c + 1) * CHUNK)
            kl = slice(h * GLA_HEAD_K, (h + 1) * GLA_HEAD_K)
            vl = slice(h * GLA_HEAD_V, (h + 1) * GLA_HEAD_V)
            state = state_s[h]
            vh = v_s[rows, vl]
            o = _dot(q_in, state.astype(BF16)) + _dot(att, vh)
            d_last = jnp.broadcast_to(decays[c][CHUNK - 1:CHUNK, kl], (GLA_HEAD_K, GLA_HEAD_K)).T
            d_last = jnp.concatenate([d_last] * (GLA_HEAD_V // GLA_HEAD_K), axis=1)
            state_s[h] = state * d_last + _dot_tn(k_out, vh)
            on = o * _rms_scale(o) * hnw_ref[...]
            g = g_s[rows, vl]
            o_s[rows, vl] = (on * (g * _sigmoid(g))).astype(BF16)

        if "project" in stages:
            x = x_ref[...]
            hn_s[...] = (x * _rms_scale(x) * nw_ref[...]).astype(BF16)

        def proj_piece(dst, first_col, j):
            cols = slice(j * PIECE_COLS, (j + 1) * PIECE_COLS)
            w_cols = slice(first_col + j * PIECE_COLS, first_col + (j + 1) * PIECE_COLS)
            def run():
                dst[:, cols] = _dot(hn_s[...], win_ref[:, w_cols]).astype(dst.dtype)
            return run

        def gate_low_piece():
            gk_s[...] = _dot(hn_s[...], wgk_ref[...]).astype(BF16)

        def gate_piece():
            logit = _dot(gk_s[...], wgu_ref[...]) + bg_ref[...]
            log_sig = jnp.minimum(logit, 0.0) - jnp.log1p(jnp.exp(-jnp.abs(logit)))
            a_n[...] = log_sig * (LOG2_E / GLA_GATE_NORMALIZER)

        def out_piece(j):
            cols = slice(j * PIECE_COLS, (j + 1) * PIECE_COLS)
            def run():
                out_ref[:, cols] = xold_ref[:, cols] + _dot(o_old[...], wout_ref[:, cols])
            return run

        pieces = []
        if "output" in stages:
            pieces += [(2, out_piece(j)) for j in range(D_MODEL // PIECE_COLS)]
        if "project" in stages:
            pieces += ([(0.25, gate_low_piece)]
                       + [(1, proj_piece(qk_n, 0, j)) for j in range(2 * GLA_DK // PIECE_COLS)]
                       + [(0.5, gate_piece)]
                       + [(1, proj_piece(v_n, 2 * GLA_DK, j)) for j in range(GLA_DV // PIECE_COLS)]
                       + [(1, proj_piece(g_n, 2 * GLA_DK + GLA_DV, j)) for j in range(GLA_DV // PIECE_COLS)])
        tasks = []
        if "recur" in stages:
            tasks = [(c, h) for c in range(n_chunks) for h in range(GLA_HEADS)]
            chunk_decays(0)
        _run_pipeline(tasks, _piece_scheduler(pieces, 2 * (len(tasks) + PIPE_LAG)), head_scores, head_output)

    _dispatch_steps(step, set_a, set_b, n_steps)


def _const_spec(arr):
    nd = arr.ndim
    return pl.BlockSpec(arr.shape, lambda i, _nd=nd: (0,) * _nd, pipeline_mode=pl.Buffered(1))


def _pipelined_call(kernel_fn, name, x, operands, buffer_set, extra_scratch, block_rows):
    seq = x.shape[0]
    t = block_rows
    nblk = seq // t
    cur_spec = pl.BlockSpec((t, D_MODEL), lambda i: (jnp.minimum(i, nblk - 1), 0))
    old_spec = pl.BlockSpec((t, D_MODEL), lambda i: (jnp.maximum(i - 2, 0), 0))
    return pl.pallas_call(
        functools.partial(kernel_fn, block_rows=t, n_steps=nblk + 2),
        grid=(nblk + 2,),
        in_specs=[cur_spec, old_spec] + [_const_spec(a) for a in operands],
        out_specs=old_spec,
        out_shape=jax.ShapeDtypeStruct((seq, D_MODEL), F32),
        scratch_shapes=buffer_set + buffer_set + extra_scratch,
        compiler_params=pltpu.CompilerParams(dimension_semantics=("arbitrary",),
                                             vmem_limit_bytes=VMEM_LIMIT_BYTES),
        name=name,
    )(x, x, *operands)


def _gla_layer(x, norm_w, w_in, w_gate_up, b_gate_up, w_head_norm, w_out, *, block_rows):
    t = block_rows
    wgk = jnp.pad(w_in[:, 2 * GLA_DK + 2 * GLA_DV:], ((0, 0), (0, LANES - GLA_GATE_RANK))).astype(BF16)
    wgu = jnp.pad(w_gate_up, ((0, LANES - GLA_GATE_RANK), (0, 0))).astype(BF16)
    operands = (norm_w.reshape(1, D_MODEL), w_in.astype(BF16), wgk, wgu, b_gate_up.reshape(1, GLA_DK),
                w_head_norm.reshape(1, GLA_HEAD_V), w_out.astype(BF16)) + _gla_constants()
    buffer_set = [
        pltpu.VMEM((t, 2 * GLA_DK), F32),
        pltpu.VMEM((t, GLA_DV), BF16),
        pltpu.VMEM((t, GLA_DV), F32),
        pltpu.VMEM((t, GLA_DK), F32),
        pltpu.VMEM((t, GLA_DV), BF16),
    ]
    extra = [
        pltpu.VMEM((t, D_MODEL), BF16),
        pltpu.VMEM((t, LANES), BF16),
        pltpu.VMEM((GLA_HEADS, GLA_HEAD_K, GLA_HEAD_V), F32),
    ]
    return _pipelined_call(_gla_kernel, "gla_layer", x, operands, buffer_set, extra, t)


def _ssd_kernel(x_ref, xold_ref, nw_ref, win_ref, wdt_ref, convw_ref, convb_ref, dtb_ref, alog_ref,
                dskip_ref, gnw_ref, wout_ref, fnw_ref,
                tri3_ref, expand_ref, eye_ref, cbias_ref,
                out_ref,
                z_a, xs_a, bc_a, acol_a, dtx_a, y_a, z_b, xs_b, bc_b, acol_b, dtx_b, y_b,
                hn_s, halo_s, da_s, heads_s, state_s, *, block_rows, n_steps):
    t = block_rows
    set_a = (z_a, xs_a, bc_a, acol_a, dtx_a, y_a)
    set_b = (z_b, xs_b, bc_b, acol_b, dtx_b, y_b)

    @pl.when(pl.program_id(0) == 0)
    def _():
        state_s[...] = jnp.zeros_like(state_s)
        halo_s[...] = jnp.zeros_like(halo_s)
        y_a[...] = jnp.zeros_like(y_a)

    def step(read_set, write_set, stages):
        z_s, xs_s, bc_s, acol_s, dtx_s, y_old = read_set
        z_n, xs_n, bc_n, acol_n, dtx_n, y_s = write_set
        n_chunks = t // CHUNK

        if "project" in stages:
            x = x_ref[...]
            hn_s[...] = (x * _rms_scale(x) * nw_ref[...]).astype(BF16)

        def z_piece(j):
            cols = slice(j * PIECE_COLS, (j + 1) * PIECE_COLS)
            def run():
                z_n[:, cols] = _dot(hn_s[...], win_ref[:, cols])
            return run

        def conv_piece(j):
            cols = slice(j * PIECE_COLS, (j + 1) * PIECE_COLS)
            w_cols = slice(D_INNER + j * PIECE_COLS, D_INNER + (j + 1) * PIECE_COLS)
            def run():
                raw = _dot(hn_s[...], win_ref[:, w_cols])
                ext = jnp.concatenate([halo_s[:, cols], raw], axis=0)
                delayed = pltpu.roll(ext, 1, 0)
                near = convw_ref[3:4, cols] * raw + convw_ref[2:3, cols] * delayed[CONV_HALO:, :]
                far = convw_ref[1:2, cols] * ext + convw_ref[0:1, cols] * delayed
                conv = convb_ref[:, cols] + near + pltpu.roll(far, 2, 0)[CONV_HALO:, :]
                halo_s[:, cols] = raw[t - CONV_HALO:, :]
                conv = conv * _sigmoid(conv)
                if j * PIECE_COLS < D_INNER:
                    xs_n[:, cols] = conv
                else:
                    bc_n[:, j * PIECE_COLS - D_INNER:(j + 1) * PIECE_COLS - D_INNER] = conv.astype(BF16)
            return run

        lane = lax.broadcasted_iota(jnp.int32, (t, LANES), 1)

        def lane_groups(v):
            v1, v2, v3 = _split3(v)
            return jnp.where(lane < SSD_HEADS, v1, jnp.where(lane < 2 * SSD_HEADS, v2, v3))

        def dt_piece():
            dt_in = _dot(hn_s[...], wdt_ref[...]) + dtb_ref[...]
            dt = jnp.maximum(dt_in, 0.0) + jnp.log1p(jnp.exp(-jnp.abs(dt_in)))
            heads_s[1] = lane_groups(dt)
            da_s[...] = dt * (-LOG2_E * jnp.exp(alog_ref[...]))

        def cumsum_piece():
            a_cum = []
            for c in range(n_chunks):
                a1, a2, a3 = _split3(da_s[c * CHUNK:(c + 1) * CHUNK, :])
                a_cum.append(_dot(tri3_ref[...], jnp.concatenate([a1, a2, a3], axis=0)))
            heads_s[0] = lane_groups(jnp.concatenate(a_cum, axis=0))

        def expand_piece(dst, src, j):
            cols = slice(j * EXPAND_COLS, (j + 1) * EXPAND_COLS)
            def run():
                dst[:, cols] = _dot(heads_s[src], expand_ref[:, cols])
            return run

        def out_piece(j):
            cols = slice(j * PIECE_COLS, (j + 1) * PIECE_COLS)
            def run():
                out_ref[:, cols] = xold_ref[:, cols] + _dot(y_old[...], wout_ref[:, cols])
            return run

        def final_norm_piece():
            res = out_ref[...]
            out_ref[...] = res * _rms_scale(res) * fnw_ref[...]

        n_exp = D_INNER // EXPAND_COLS
        streams = []
        if "output" in stages:
            streams += [[(2, out_piece(j)) for j in range(D_MODEL // PIECE_COLS)] + [(0.25, final_norm_piece)]]
        if "project" in stages:
            streams += [[(1, conv_piece(j)) for j in range(SSD_CONV_DIM // PIECE_COLS)],
                        [(1, z_piece(j)) for j in range(D_INNER // PIECE_COLS)],
                        [(0.25, dt_piece), (0.25, cumsum_piece)]
                        + [(0.5, expand_piece(acol_n, 0, j)) for j in range(n_exp)]
                        + [(0.5, expand_piece(dtx_n, 1, j)) for j in range(n_exp)]]
        pieces = _merge_evenly(*streams)

        prep = {}
        lane2 = lax.broadcasted_iota(jnp.int32, (CHUNK, LANES), 1)

        def group_prep(c, g):
            rows = slice(c * CHUNK, (c + 1) * CHUNK)
            gl = slice(g * SSD_GROUP_WIDTH, (g + 1) * SSD_GROUP_WIDTH)
            acol = acol_s[rows, gl]
            xs = xs_s[rows, gl]
            xdt = xs * dtx_s[rows, gl]
            arow = jnp.sum(acol * eye_ref[:, gl], axis=0, keepdims=True)
            a_last = acol[CHUNK - 1:CHUNK, :]
            prep[(c, g)] = dict(
                xs=xs,
                lmat=jnp.exp2(acol - arow + cbias_ref[:, gl]),
                from_start=jnp.exp2(acol),
                xw=(xdt * jnp.exp2(a_last - acol)).astype(BF16),
                xdt_b=xdt.astype(BF16),
                chunk_decay=jnp.exp2(a_last))

        def group_scores(c, g):
            if c + 1 < n_chunks:
                group_prep(c + 1, g)
            rows = slice(c * CHUNK, (c + 1) * CHUNK)
            b_g = bc_s[rows, g * SSD_STATE:(g + 1) * SSD_STATE]
            c_g = bc_s[rows, (SSD_GROUPS + g) * SSD_STATE:(SSD_GROUPS + g + 1) * SSD_STATE]
            cb2 = _dot_nt(c_g, jnp.concatenate([b_g, b_g], axis=0))
            return b_g, c_g, cb2

        def group_output(c, g, scores):
            b_g, c_g, cb2 = scores
            p = prep.pop((c, g))
            rows = slice(c * CHUNK, (c + 1) * CHUNK)
            gl = slice(g * SSD_GROUP_WIDTH, (g + 1) * SSD_GROUP_WIDTH)
            state = state_s[g]
            y_g = _dot(c_g, state.astype(BF16)) * p["from_start"]
            diag = []
            for half in range(SSD_GROUP_WIDTH // LANES):
                tl = slice(half * LANES, (half + 1) * LANES)
                m_pair = (cb2 * p["lmat"][:, tl]).astype(BF16)
                x_pair = p["xdt_b"][:, tl]
                zero = jnp.zeros_like(x_pair)
                rhs = jnp.concatenate([jnp.where(lane2 < SSD_HEAD_DIM, x_pair, zero),
                                       jnp.where(lane2 < SSD_HEAD_DIM, zero, x_pair)], axis=0)
                diag.append(_dot(m_pair, rhs))
            state_s[g] = state * p["chunk_decay"] + _dot_tn(b_g, p["xw"])
            y_g = y_g + jnp.concatenate(diag, axis=1) + p["xs"] * dskip_ref[:, gl]
            z = z_s[rows, gl]
            y_g = y_g * (z * _sigmoid(z))
            gated[(c, g)] = y_g
            sumsq[(c, g)] = jnp.sum(y_g * y_g, axis=-1, keepdims=True)
            if g == SSD_GROUPS - 1:
                total = sumsq.pop((c, 0))
                for k in range(1, SSD_GROUPS):
                    total = total + sumsq.pop((c, k))
                scale = lax.rsqrt(total * (1.0 / D_INNER) + RMS_EPS)
                for k in range(SSD_GROUPS):
                    kl = slice(k * SSD_GROUP_WIDTH, (k + 1) * SSD_GROUP_WIDTH)
                    y_s[rows, kl] = (gated.pop((c, k)) * scale * gnw_ref[:, kl]).astype(BF16)

        gated, sumsq = {}, {}
        tasks = []
        if "recur" in stages:
            tasks = [(c, g) for c in range(n_chunks) for g in range(SSD_GROUPS)]
            for g in range(SSD_GROUPS):
                group_prep(0, g)
        _run_pipeline(tasks, _piece_scheduler(pieces, 2 * (len(tasks) + PIPE_LAG)), group_scores, group_output)

    _dispatch_steps(step, set_a, set_b, n_steps)


def _ssd_layer(x, norm_w, w_in, conv_w, conv_b, dt_bias, a_log, d_skip, w_gate_norm, w_out, final_norm,
               *, block_rows):
    t = block_rows
    w_dt = w_in[:, D_INNER + SSD_CONV_DIM:]
    pad = jnp.zeros((D_MODEL, LANES - 3 * SSD_HEADS), w_dt.dtype)
    wdt = jnp.concatenate([w_dt, w_dt, w_dt, pad], axis=1).astype(BF16)

    def head_lanes(v):
        return jnp.concatenate([v, v, v, jnp.zeros((LANES - 3 * SSD_HEADS,), v.dtype)]).reshape(1, LANES)

    operands = (norm_w.reshape(1, D_MODEL), w_in.astype(BF16), wdt, conv_w, conv_b.reshape(1, SSD_CONV_DIM),
                head_lanes(dt_bias), head_lanes(a_log),
                jnp.repeat(d_skip, SSD_HEAD_DIM).reshape(1, D_INNER),
                w_gate_norm.reshape(1, D_INNER), w_out.astype(BF16),
                final_norm.reshape(1, D_MODEL)) + _ssd_constants()
    buffer_set = [
        pltpu.VMEM((t, D_INNER), F32),
        pltpu.VMEM((t, D_INNER), F32),
        pltpu.VMEM((t, 2 * SSD_GROUPS * SSD_STATE), BF16),
        pltpu.VMEM((t, D_INNER), F32),
        pltpu.VMEM((t, D_INNER), F32),
        pltpu.VMEM((t, D_INNER), BF16),
    ]
    extra = [
        pltpu.VMEM((t, D_MODEL), BF16),
        pltpu.VMEM((CONV_HALO, SSD_CONV_DIM), F32),
        pltpu.VMEM((t, LANES), F32),
        pltpu.VMEM((2, t, LANES), BF16),
        pltpu.VMEM((SSD_GROUPS, SSD_STATE, SSD_GROUP_WIDTH), F32),
    ]
    return _pipelined_call(_ssd_kernel, "ssd_layer", x, operands, buffer_set, extra, t)


def kernel(x, norm_w, gla_in_proj, gla_gate_up, gla_gate_bias, gla_head_norm, gla_out_proj, ssd_in_proj,
           ssd_conv_w, ssd_conv_b, ssd_dt_bias, ssd_a_log, ssd_d, ssd_gate_norm, ssd_out_proj, final_norm):
    bsz, seq, _ = x.shape
    assert bsz == 1 and seq % BLOCK_ROWS == 0
    h = x.reshape(seq, D_MODEL)
    h = _gla_layer(h, norm_w[0], gla_in_proj[0], gla_gate_up[0], gla_gate_bias[0], gla_head_norm[0],
                   gla_out_proj[0], block_rows=BLOCK_ROWS)
    h = _ssd_layer(h, norm_w[1], ssd_in_proj[0], ssd_conv_w[0], ssd_conv_b[0], ssd_dt_bias[0], ssd_a_log[0],
                   ssd_d[0], ssd_gate_norm[0], ssd_out_proj[0], final_norm, block_rows=BLOCK_ROWS)
    return h.reshape(bsz, seq, D_MODEL)
```

```python
import functools

import numpy as np
import jax
import jax.numpy as jnp
from jax import lax
from jax.experimental import pallas as pl
from jax.experimental.pallas import tpu as pltpu

F32 = jnp.float32
BF16 = jnp.bfloat16

D_MODEL = 1024
D_INNER = 2048
RMS_EPS = 1e-6
LOG2_E = 1.4426950408889634
CHUNK = 64

GLA_HEADS = 4
GLA_DK = 512
GLA_DV = 2048
GLA_HEAD_K = 128
GLA_HEAD_V = 512
GLA_GATE_RANK = 16
GLA_GATE_NORMALIZER = 16.0
GLA_LEVELS = (32, 16, 8, 4, 2, 1)

SSD_HEAD_DIM = 64
SSD_HEADS = 32
SSD_GROUPS = 8
SSD_STATE = 128
SSD_CONV = 4
SSD_CONV_DIM = 4096
SSD_GROUP_WIDTH = (SSD_HEADS // SSD_GROUPS) * SSD_HEAD_DIM

LANES = 128
CONV_HALO = 8
BLOCK_ROWS = 256
PIECE_COLS = 256
EXPAND_COLS = 512
PIPE_LAG = 1
VMEM_LIMIT_BYTES = 60 * 1024 * 1024


def _dot(a, b):
    return jnp.dot(a, b, preferred_element_type=F32)


def _dot_nt(a, b):
    return lax.dot_general(a, b, (((1,), (1,)), ((), ())), preferred_element_type=F32)


def _dot_tn(a, b):
    return lax.dot_general(a, b, (((0,), (0,)), ((), ())), preferred_element_type=F32)


def _split3(x):
    x1 = x.astype(BF16)
    r1 = x - x1.astype(F32)
    x2 = r1.astype(BF16)
    x3 = (r1 - x2.astype(F32)).astype(BF16)
    return x1, x2, x3


def _rms_scale(x):
    return lax.rsqrt(jnp.mean(x * x, axis=-1, keepdims=True) + RMS_EPS)


def _sigmoid(x):
    return 1.0 / (1.0 + jnp.exp2(x * -LOG2_E))


def _piece_scheduler(pieces, n_calls):
    total = sum(w for w, _ in pieces)
    state = {"call": 0, "next": 0, "done": 0}

    def run():
        state["call"] += 1
        target = total * state["call"] / n_calls
        while state["next"] < len(pieces) and (state["done"] < target or state["call"] == n_calls):
            w, fn = pieces[state["next"]]
            fn()
            state["done"] += w
            state["next"] += 1
    return run


def _merge_evenly(*streams):
    keyed = [((k + (0.5 if si else 0.0)) / len(s), si, k) for si, s in enumerate(streams) for k in range(len(s))]
    return [streams[si][k] for _, si, k in sorted(keyed)]


def _run_pipeline(tasks, run_pieces, scores_fn, output_fn):
    n_slots = len(tasks) + PIPE_LAG
    scores = {}
    for slot in range(n_slots):
        run_pieces()
        if slot < len(tasks):
            scores[slot] = scores_fn(*tasks[slot])
        run_pieces()
        if slot >= PIPE_LAG:
            output_fn(*tasks[slot - PIPE_LAG], scores.pop(slot - PIPE_LAG))


ALL_STAGES = ("project", "recur", "output")


def _dispatch_steps(step_fn, set_a, set_b, n_steps):
    step_id = pl.program_id(0)
    last = n_steps - 1
    sets = lambda parity: (set_b, set_a) if parity == 0 else (set_a, set_b)

    @pl.when(step_id == 0)
    def _():
        step_fn(*sets(0), ("project",))

    @pl.when(step_id == last)
    def _():
        step_fn(*sets(last % 2), ("output",))

    for parity in (0, 1):
        @pl.when((step_id % 2 == parity) & (step_id > 0) & (step_id < last))
        def _():
            step_fn(*sets(parity), ALL_STAGES)


def _gla_constants():
    c = CHUNK
    i = np.arange(c)[:, None]
    m = np.arange(c)[None, :]
    sets = [m <= i, m > i]
    roles_q, roles_k, pair_masks = [], [], [(i == m)]
    for s in GLA_LEVELS:
        blk = i // s
        lower = (blk % 2) == 1
        prefix = (m >= blk * s) & (m <= i)
        suffix = (m > i) & (m < (blk + 1) * s)
        pair_masks.append(lower & ((m // s) == blk - 1))
        if s > 1:
            sets.append(np.where(lower, prefix, suffix))
            roles_q.append(np.broadcast_to(lower, (c, LANES)))
            roles_k.append(np.broadcast_to(~lower, (c, LANES)))
    sum_mat = np.concatenate(sets, axis=0).astype(np.float32)
    sum_mat = np.concatenate([sum_mat] * 3, axis=1)
    return (jnp.asarray(sum_mat, BF16),
            jnp.asarray(np.stack(roles_q), F32), jnp.asarray(np.stack(roles_k), F32),
            jnp.asarray(np.stack(pair_masks), F32))


def _ssd_constants():
    c = CHUNK
    i = np.arange(c)[:, None]
    m = np.arange(c)[None, :]
    tri = (m <= i).astype(np.float32)
    tri3 = np.concatenate([tri] * 3, axis=1)
    expand = np.zeros((LANES, D_INNER), np.float32)
    for part in range(3):
        for h in range(SSD_HEADS):
            expand[part * SSD_HEADS + h, h * SSD_HEAD_DIM:(h + 1) * SSD_HEAD_DIM] = 1.0
    col = np.arange(D_INNER)[None, :] % SSD_HEAD_DIM
    eye_tiled = (col == i).astype(np.float32)
    causal_tiled = (col <= i)
    causal_bias = np.where(causal_tiled, 0.0, -np.inf).astype(np.float32)
    return (jnp.asarray(tri3, BF16), jnp.asarray(expand, BF16),
            jnp.asarray(eye_tiled, F32), jnp.asarray(causal_bias, F32))


def _gla_kernel(x_ref, xold_ref, nw_ref, win_ref, wgk_ref, wgu_ref, bg_ref, hnw_ref, wout_ref,
                summat_ref, roleq_ref, rolek_ref, pmask_ref,
                out_ref,
                qk_a, v_a, g_a, a_a, o_a, qk_b, v_b, g_b, a_b, o_b, hn_s, gk_s, state_s, *, block_rows, n_steps):
    set_a = (qk_a, v_a, g_a, a_a, o_a)
    set_b = (qk_b, v_b, g_b, a_b, o_b)

    @pl.when(pl.program_id(0) == 0)
    def _():
        state_s[...] = jnp.zeros_like(state_s)
        o_a[...] = jnp.zeros_like(o_a)

    def step(read_set, write_set, stages):
        qk_s, v_s, g_s, a_s, o_old = read_set
        qk_n, v_n, g_n, a_n, o_s = write_set
        q_scale = GLA_HEAD_K ** -0.5
        n_chunks = block_rows // CHUNK
        decays = {}

        def chunk_decays(c):
            rows = slice(c * CHUNK, (c + 1) * CHUNK)
            a1, a2, a3 = _split3(a_s[rows, :])
            expo = _dot(summat_ref[...], jnp.concatenate([a1, a2, a3], axis=0))
            decays[c] = jnp.exp2(expo)

        def head_scores(c, h):
            if h == 0 and c + 1 < n_chunks:
                chunk_decays(c + 1)
            rows = slice(c * CHUNK, (c + 1) * CHUNK)
            kl = slice(h * GLA_HEAD_K, (h + 1) * GLA_HEAD_K)
            decay = decays[c]
            qh = qk_s[rows, kl] * q_scale
            kh = qk_s[rows, GLA_DK + h * GLA_HEAD_K:GLA_DK + (h + 1) * GLA_HEAD_K]
            att = pmask_ref[0] * jnp.sum(qh * kh, axis=-1, keepdims=True)
            k_prev = pltpu.roll(kh, 1, 0)
            pair = jnp.sum(qh * jnp.exp2(a_s[rows, kl]) * k_prev, axis=-1, keepdims=True)
            att = att + pmask_ref[len(GLA_LEVELS)] * pair
            for lv in range(len(GLA_LEVELS) - 1):
                d_lv = decay[(2 + lv) * CHUNK:(3 + lv) * CHUNK, kl]
                fq = (qh * (d_lv * roleq_ref[lv])).astype(BF16)
                fk = (kh * (d_lv * rolek_ref[lv])).astype(BF16)
                att = att + pmask_ref[lv + 1] * _dot_nt(fq, fk)
            q_in = (qh * decay[0:CHUNK, kl]).astype(BF16)
            k_out = (kh * decay[CHUNK:2 * CHUNK, kl]).astype(BF16)
            return att.astype(BF16), q_in, k_out

        def head_output(c, h, scores):
            att, q_in, k_out = scores
            rows = slice(c * CHUNK, (c + 1) * CHUNK)
            kl = slice(h * GLA_HEAD_K, (h + 1) * GLA_HEAD_K)
            vl = slice(h * GLA_HEAD_V, (h + 1) * GLA_HEAD_V)
            state = state_s[h]
            vh = v_s[rows, vl]
            o = _dot(q_in, state.astype(BF16)) + _dot(att, vh)
            d_last = jnp.broadcast_to(decays[c][CHUNK - 1:CHUNK, kl], (GLA_HEAD_K, GLA_HEAD_K)).T
            d_last = jnp.concatenate([d_last] * (GLA_HEAD_V // GLA_HEAD_K), axis=1)
            state_s[h] = state * d_last + _dot_tn(k_out, vh)
            on = o * _rms_scale(o) * hnw_ref[...]
            g = g_s[rows, vl]
            o_s[rows, vl] = (on * (g * _sigmoid(g))).astype(BF16)

        if "project" in stages:
            x = x_ref[...]
            hn_s[...] = (x * _rms_scale(x) * nw_ref[...]).astype(BF16)

        def proj_piece(dst, first_col, j):
            cols = slice(j * PIECE_COLS, (j + 1) * PIECE_COLS)
            w_cols = slice(first_col + j * PIECE_COLS, first_col + (j + 1) * PIECE_COLS)
            def run():
                dst[:, cols] = _dot(hn_s[...], win_ref[:, w_cols]).astype(dst.dtype)
            return run

        def gate_low_piece():
            gk_s[...] = _dot(hn_s[...], wgk_ref[...]).astype(BF16)

        def gate_piece():
            logit = _dot(gk_s[...], wgu_ref[...]) + bg_ref[...]
            log_sig = jnp.minimum(logit, 0.0) - jnp.log1p(jnp.exp(-jnp.abs(logit)))
            a_n[...] = log_sig * (LOG2_E / GLA_GATE_NORMALIZER)

        def out_piece(j):
            cols = slice(j * PIECE_COLS, (j + 1) * PIECE_COLS)
            def run():
                out_ref[:, cols] = xold_ref[:, cols] + _dot(o_old[...], wout_ref[:, cols])
            return run

        pieces = []
        if "output" in stages:
            pieces += [(2, out_piece(j)) for j in range(D_MODEL // PIECE_COLS)]
        if "project" in stages:
            pieces += ([(0.25, gate_low_piece)]
                       + [(1, proj_piece(qk_n, 0, j)) for j in range(2 * GLA_DK // PIECE_COLS)]
                       + [(0.5, gate_piece)]
                       + [(1, proj_piece(v_n, 2 * GLA_DK, j)) for j in range(GLA_DV // PIECE_COLS)]
                       + [(1, proj_piece(g_n, 2 * GLA_DK + GLA_DV, j)) for j in range(GLA_DV // PIECE_COLS)])
        tasks = []
        if "recur" in stages:
            tasks = [(c, h) for c in range(n_chunks) for h in range(GLA_HEADS)]
            chunk_decays(0)
        _run_pipeline(tasks, _piece_scheduler(pieces, 2 * (len(tasks) + PIPE_LAG)), head_scores, head_output)

    _dispatch_steps(step, set_a, set_b, n_steps)


def _const_spec(arr):
    nd = arr.ndim
    return pl.BlockSpec(arr.shape, lambda i, _nd=nd: (0,) * _nd, pipeline_mode=pl.Buffered(1))


def _pipelined_call(kernel_fn, name, x, operands, buffer_set, extra_scratch, block_rows):
    seq = x.shape[0]
    t = block_rows
    nblk = seq // t
    cur_spec = pl.BlockSpec((t, D_MODEL), lambda i: (jnp.minimum(i, nblk - 1), 0))
    old_spec = pl.BlockSpec((t, D_MODEL), lambda i: (jnp.maximum(i - 2, 0), 0))
    return pl.pallas_call(
        functools.partial(kernel_fn, block_rows=t, n_steps=nblk + 2),
        grid=(nblk + 2,),
        in_specs=[cur_spec, old_spec] + [_const_spec(a) for a in operands],
        out_specs=old_spec,
        out_shape=jax.ShapeDtypeStruct((seq, D_MODEL), F32),
        scratch_shapes=buffer_set + buffer_set + extra_scratch,
        compiler_params=pltpu.CompilerParams(dimension_semantics=("arbitrary",),
                                             vmem_limit_bytes=VMEM_LIMIT_BYTES),
        name=name,
    )(x, x, *operands)


def _gla_layer(x, norm_w, w_in, w_gate_up, b_gate_up, w_head_norm, w_out, *, block_rows):
    t = block_rows
    wgk = jnp.pad(w_in[:, 2 * GLA_DK + 2 * GLA_DV:], ((0, 0), (0, LANES - GLA_GATE_RANK))).astype(BF16)
    wgu = jnp.pad(w_gate_up, ((0, LANES - GLA_GATE_RANK), (0, 0))).astype(BF16)
    operands = (norm_w.reshape(1, D_MODEL), w_in.astype(BF16), wgk, wgu, b_gate_up.reshape(1, GLA_DK),
                w_head_norm.reshape(1, GLA_HEAD_V), w_out.astype(BF16)) + _gla_constants()
    buffer_set = [
        pltpu.VMEM((t, 2 * GLA_DK), F32),
        pltpu.VMEM((t, GLA_DV), BF16),
        pltpu.VMEM((t, GLA_DV), F32),
        pltpu.VMEM((t, GLA_DK), F32),
        pltpu.VMEM((t, GLA_DV), BF16),
    ]
    extra = [
        pltpu.VMEM((t, D_MODEL), BF16),
        pltpu.VMEM((t, LANES), BF16),
        pltpu.VMEM((GLA_HEADS, GLA_HEAD_K, GLA_HEAD_V), F32),
    ]
    return _pipelined_call(_gla_kernel, "gla_layer", x, operands, buffer_set, extra, t)


def _ssd_kernel(x_ref, xold_ref, nw_ref, win_ref, wdt_ref, convw_ref, convb_ref, dtb_ref, alog_ref,
                dskip_ref, gnw_ref, wout_ref, fnw_ref,
                tri3_ref, expand_ref, eye_ref, cbias_ref,
                out_ref,
                z_a, xs_a, bc_a, acol_a, dtx_a, y_a, z_b, xs_b, bc_b, acol_b, dtx_b, y_b,
                hn_s, halo_s, da_s, heads_s, state_s, *, block_rows, n_steps):
    t = block_rows
    set_a = (z_a, xs_a, bc_a, acol_a, dtx_a, y_a)
    set_b = (z_b, xs_b, bc_b, acol_b, dtx_b, y_b)

    @pl.when(pl.program_id(0) == 0)
    def _():
        state_s[...] = jnp.zeros_like(state_s)
        halo_s[...] = jnp.zeros_like(halo_s)
        y_a[...] = jnp.zeros_like(y_a)

    def step(read_set, write_set, stages):
        z_s, xs_s, bc_s, acol_s, dtx_s, y_old = read_set
        z_n, xs_n, bc_n, acol_n, dtx_n, y_s = write_set
        n_chunks = t // CHUNK

        if "project" in stages:
            x = x_ref[...]
            hn_s[...] = (x * _rms_scale(x) * nw_ref[...]).astype(BF16)

        def z_piece(j):
            cols = slice(j * PIECE_COLS, (j + 1) * PIECE_COLS)
            def run():
                z_n[:, cols] = _dot(hn_s[...], win_ref[:, cols])
            return run

        def conv_piece(j):
            cols = slice(j * PIECE_COLS, (j + 1) * PIECE_COLS)
            w_cols = slice(D_INNER + j * PIECE_COLS, D_INNER + (j + 1) * PIECE_COLS)
            def run():
                raw = _dot(hn_s[...], win_ref[:, w_cols])
                ext = jnp.concatenate([halo_s[:, cols], raw], axis=0)
                delayed = pltpu.roll(ext, 1, 0)
                near = convw_ref[3:4, cols] * raw + convw_ref[2:3, cols] * delayed[CONV_HALO:, :]
                far = convw_ref[1:2, cols] * ext + convw_ref[0:1, cols] * delayed
                conv = convb_ref[:, cols] + near + pltpu.roll(far, 2, 0)[CONV_HALO:, :]
                halo_s[:, cols] = raw[t - CONV_HALO:, :]
                conv = conv * _sigmoid(conv)
                if j * PIECE_COLS < D_INNER:
                    xs_n[:, cols] = conv
                else:
                    bc_n[:, j * PIECE_COLS - D_INNER:(j + 1) * PIECE_COLS - D_INNER] = conv.astype(BF16)
            return run

        lane = lax.broadcasted_iota(jnp.int32, (t, LANES), 1)

        def lane_groups(v):
            v1, v2, v3 = _split3(v)
            return jnp.where(lane < SSD_HEADS, v1, jnp.where(lane < 2 * SSD_HEADS, v2, v3))

        def dt_piece():
            dt_in = _dot(hn_s[...], wdt_ref[...]) + dtb_ref[...]
            dt = jnp.maximum(dt_in, 0.0) + jnp.log1p(jnp.exp(-jnp.abs(dt_in)))
            heads_s[1] = lane_groups(dt)
            da_s[...] = dt * (-LOG2_E * jnp.exp(alog_ref[...]))

        def cumsum_piece():
            a_cum = []
            for c in range(n_chunks):
                a1, a2, a3 = _split3(da_s[c * CHUNK:(c + 1) * CHUNK, :])
                a_cum.append(_dot(tri3_ref[...], jnp.concatenate([a1, a2, a3], axis=0)))
            heads_s[0] = lane_groups(jnp.concatenate(a_cum, axis=0))

        def expand_piece(dst, src, j):
            cols = slice(j * EXPAND_COLS, (j + 1) * EXPAND_COLS)
            def run():
                dst[:, cols] = _dot(heads_s[src], expand_ref[:, cols])
            return run

        def out_piece(j):
            cols = slice(j * PIECE_COLS, (j + 1) * PIECE_COLS)
            def run():
                out_ref[:, cols] = xold_ref[:, cols] + _dot(y_old[...], wout_ref[:, cols])
            return run

        def final_norm_piece():
            res = out_ref[...]
            out_ref[...] = res * _rms_scale(res) * fnw_ref[...]

        n_exp = D_INNER // EXPAND_COLS
        streams = []
        if "output" in stages:
            streams += [[(2, out_piece(j)) for j in range(D_MODEL // PIECE_COLS)] + [(0.25, final_norm_piece)]]
        if "project" in stages:
            streams += [[(1, conv_piece(j)) for j in range(SSD_CONV_DIM // PIECE_COLS)],
                        [(1, z_piece(j)) for j in range(D_INNER // PIECE_COLS)],
                        [(0.25, dt_piece), (0.25, cumsum_piece)]
                        + [(0.5, expand_piece(acol_n, 0, j)) for j in range(n_exp)]
                        + [(0.5, expand_piece(dtx_n, 1, j)) for j in range(n_exp)]]
        pieces = _merge_evenly(*streams)

        prep = {}
        lane2 = lax.broadcasted_iota(jnp.int32, (CHUNK, LANES), 1)

        def group_prep(c, g):
            rows = slice(c * CHUNK, (c + 1) * CHUNK)
            gl = slice(g * SSD_GROUP_WIDTH, (g + 1) * SSD_GROUP_WIDTH)
            acol = acol_s[rows, gl]
            xs = xs_s[rows, gl]
            xdt = xs * dtx_s[rows, gl]
            arow = jnp.sum(acol * eye_ref[:, gl], axis=0, keepdims=True)
            a_last = acol[CHUNK - 1:CHUNK, :]
            prep[(c, g)] = dict(
                xs=xs,
                lmat=jnp.exp2(acol - arow + cbias_ref[:, gl]),
                from_start=jnp.exp2(acol),
                xw=(xdt * jnp.exp2(a_last - acol)).astype(BF16),
                xdt_b=xdt.astype(BF16),
                chunk_decay=jnp.exp2(a_last))

        def group_scores(c, g):
            if c + 1 < n_chunks:
                group_prep(c + 1, g)
            rows = slice(c * CHUNK, (c + 1) * CHUNK)
            b_g = bc_s[rows, g * SSD_STATE:(g + 1) * SSD_STATE]
            c_g = bc_s[rows, (SSD_GROUPS + g) * SSD_STATE:(SSD_GROUPS + g + 1) * SSD_STATE]
            cb2 = _dot_nt(c_g, jnp.concatenate([b_g, b_g], axis=0))
            return b_g, c_g, cb2

        def group_output(c, g, scores):
            b_g, c_g, cb2 = scores
            p = prep.pop((c, g))
            rows = slice(c * CHUNK, (c + 1) * CHUNK)
            gl = slice(g * SSD_GROUP_WIDTH, (g + 1) * SSD_GROUP_WIDTH)
            state = state_s[g]
            y_g = _dot(c_g, state.astype(BF16)) * p["from_start"]
            diag = []
            for half in range(SSD_GROUP_WIDTH // LANES):
                tl = slice(half * LANES, (half + 1) * LANES)
                m_pair = (cb2 * p["lmat"][:, tl]).astype(BF16)
                x_pair = p["xdt_b"][:, tl]
                zero = jnp.zeros_like(x_pair)
                rhs = jnp.concatenate([jnp.where(lane2 < SSD_HEAD_DIM, x_pair, zero),
                                       jnp.where(lane2 < SSD_HEAD_DIM, zero, x_pair)], axis=0)
                diag.append(_dot(m_pair, rhs))
            state_s[g] = state * p["chunk_decay"] + _dot_tn(b_g, p["xw"])
            y_g = y_g + jnp.concatenate(diag, axis=1) + p["xs"] * dskip_ref[:, gl]
            z = z_s[rows, gl]
            y_g = y_g * (z * _sigmoid(z))
            gated[(c, g)] = y_g
            sumsq[(c, g)] = jnp.sum(y_g * y_g, axis=-1, keepdims=True)
            if g == SSD_GROUPS - 1:
                total = sumsq.pop((c, 0))
                for k in range(1, SSD_GROUPS):
                    total = total + sumsq.pop((c, k))
                scale = lax.rsqrt(total * (1.0 / D_INNER) + RMS_EPS)
                for k in range(SSD_GROUPS):
                    kl = slice(k * SSD_GROUP_WIDTH, (k + 1) * SSD_GROUP_WIDTH)
                    y_s[rows, kl] = (gated.pop((c, k)) * scale * gnw_ref[:, kl]).astype(BF16)

        gated, sumsq = {}, {}
        tasks = []
        if "recur" in stages:
            tasks = [(c, g) for c in range(n_chunks) for g in range(SSD_GROUPS)]
            for g in range(SSD_GROUPS):
                group_prep(0, g)
        _run_pipeline(tasks, _piece_scheduler(pieces, 2 * (len(tasks) + PIPE_LAG)), group_scores, group_output)

    _dispatch_steps(step, set_a, set_b, n_steps)


def _ssd_layer(x, norm_w, w_in, conv_w, conv_b, dt_bias, a_log, d_skip, w_gate_norm, w_out, final_norm,
               *, block_rows):
    t = block_rows
    w_dt = w_in[:, D_INNER + SSD_CONV_DIM:]
    pad = jnp.zeros((D_MODEL, LANES - 3 * SSD_HEADS), w_dt.dtype)
    wdt = jnp.concatenate([w_dt, w_dt, w_dt, pad], axis=1).astype(BF16)

    def head_lanes(v):
        return jnp.concatenate([v, v, v, jnp.zeros((LANES - 3 * SSD_HEADS,), v.dtype)]).reshape(1, LANES)

    operands = (norm_w.reshape(1, D_MODEL), w_in.astype(BF16), wdt, conv_w, conv_b.reshape(1, SSD_CONV_DIM),
                head_lanes(dt_bias), head_lanes(a_log),
                jnp.repeat(d_skip, SSD_HEAD_DIM).reshape(1, D_INNER),
                w_gate_norm.reshape(1, D_INNER), w_out.astype(BF16),
                final_norm.reshape(1, D_MODEL)) + _ssd_constants()
    buffer_set = [
        pltpu.VMEM((t, D_INNER), F32),
        pltpu.VMEM((t, D_INNER), F32),
        pltpu.VMEM((t, 2 * SSD_GROUPS * SSD_STATE), BF16),
        pltpu.VMEM((t, D_INNER), F32),
        pltpu.VMEM((t, D_INNER), F32),
        pltpu.VMEM((t, D_INNER), BF16),
    ]
    extra = [
        pltpu.VMEM((t, D_MODEL), BF16),
        pltpu.VMEM((CONV_HALO, SSD_CONV_DIM), F32),
        pltpu.VMEM((t, LANES), F32),
        pltpu.VMEM((2, t, LANES), BF16),
        pltpu.VMEM((SSD_GROUPS, SSD_STATE, SSD_GROUP_WIDTH), F32),
    ]
    return _pipelined_call(_ssd_kernel, "ssd_layer", x, operands, buffer_set, extra, t)


def kernel(x, norm_w, gla_in_proj, gla_gate_up, gla_gate_bias, gla_head_norm, gla_out_proj, ssd_in_proj,
           ssd_conv_w, ssd_conv_b, ssd_dt_bias, ssd_a_log, ssd_d, ssd_gate_norm, ssd_out_proj, final_norm):
    bsz, seq, _ = x.shape
    assert bsz == 1 and seq % BLOCK_ROWS == 0
    h = x.reshape(seq, D_MODEL)
    h = _gla_layer(h, norm_w[0], gla_in_proj[0], gla_gate_up[0], gla_gate_bias[0], gla_head_norm[0],
                   gla_out_proj[0], block_rows=BLOCK_ROWS)
    h = _ssd_layer(h, norm_w[1], ssd_in_proj[0], ssd_conv_w[0], ssd_conv_b[0], ssd_dt_bias[0], ssd_a_log[0],
                   ssd_d[0], ssd_gate_norm[0], ssd_out_proj[0], final_norm, block_rows=BLOCK_ROWS)
    return h.reshape(bsz, seq, D_MODEL)
```
